```python
import math
import jax, jax.numpy as jnp
from jax import lax
import numpy as np

D_MODEL = 2048
BATCH = 4
SEQ = 4096
DEPTH = 1

CONV_CH = D_MODEL
CONV_WIDTH = 31

SSM_EXPAND = 2
D_INNER = SSM_EXPAND * D_MODEL
HEADDIM = 64
N_HEADS = D_INNER // HEADDIM
D_STATE = 128
N_GROUPS = 8
SSM_CONV = 4
CHUNK = 128
XBC_DIM = D_INNER + 2 * N_GROUPS * D_STATE

NORM_EPS = 1e-6
LN_EPS = 1e-5
DT_MIN = 0.001
DT_MAX = 0.1

IN_SIZES = (CONV_CH, CONV_CH, CONV_CH,
            D_INNER, XBC_DIM, N_HEADS,
            D_MODEL, D_MODEL)
D_IN_PROJ = sum(IN_SIZES)
SPLITS = tuple(int(v) for v in np.cumsum(IN_SIZES)[:-1])

kernel_name = "hybrid_conformer_ssd_gated_block"


def rmsnorm(x, w, eps=NORM_EPS):
    xf = x.astype(jnp.float32)
    y = xf * lax.rsqrt(jnp.mean(xf * xf, axis=-1, keepdims=True) + eps)
    return y.astype(x.dtype) * w


def causal_depthwise_conv(u, w, b):
    k = w.shape[0]
    out = lax.conv_general_dilated(
        u, w[:, None, :].astype(u.dtype), window_strides=(1,), padding=[(k - 1, 0)],
        dimension_numbers=("NWC", "WIO", "NWC"), feature_group_count=u.shape[-1])
    return out + b


def conformer_conv_branch(a_val, a_gate, a_z, conv_w, conv_b, ln_w, ln_b, w_out):
    u = a_val * jax.nn.sigmoid(a_gate)
    u = causal_depthwise_conv(u, conv_w, conv_b)
    uf = u.astype(jnp.float32)
    mu = jnp.mean(uf, axis=-1, keepdims=True)
    var = jnp.mean(jnp.square(uf - mu), axis=-1, keepdims=True)
    u = ((uf - mu) * lax.rsqrt(var + LN_EPS)).astype(u.dtype) * ln_w + ln_b
    u = jax.nn.silu(u) * jax.nn.silu(a_z)
    return u @ w_out


def ssd_chunked(xs, dt, a, bmat, cmat):
    b, s, h, p = xs.shape
    g, n = bmat.shape[-2:]
    r = h // g
    nc = s // CHUNK

    def to_chunks(t):
        return jnp.moveaxis(t.reshape(b, nc, CHUNK, *t.shape[2:]), 1, 0)

    xg = to_chunks(xs.reshape(b, s, g, r, p))
    dtg = to_chunks(dt.reshape(b, s, g, r))
    dag = dtg * a.reshape(g, r)
    bc_all = to_chunks(bmat)
    cc_all = to_chunks(cmat)
    causal = jnp.tril(jnp.ones((CHUNK, CHUNK), dtype=bool))[None, :, :, None, None]

    def step(state, inp):
        xc, dtc, dac, bc, cc = inp
        cum = jnp.cumsum(dac, axis=1)
        seg = cum[:, :, None] - cum[:, None]
        decay = jnp.exp(jnp.where(causal, seg, -jnp.inf))
        cb = jnp.einsum("blgn,bsgn->blsg", cc, bc)
        m = cb[..., None] * decay * dtc[:, None]
        y_diag = jnp.einsum("blsgr,bsgrp->blgrp", m, xc)
        y_off = jnp.einsum("blgn,bgrpn->blgrp", cc, state) * jnp.exp(cum)[..., None]
        to_end = jnp.exp(cum[:, -1:] - cum) * dtc
        new_state = (state * jnp.exp(cum[:, -1])[..., None, None]
                     + jnp.einsum("bsgn,bsgr,bsgrp->bgrpn", bc, to_end, xc))
        return new_state, y_diag + y_off

    init = jnp.zeros((b, g, r, p, n), jnp.float32)
    _, y = lax.scan(step, init, (xg, dtg, dag, bc_all, cc_all))
    return jnp.moveaxis(y, 0, 1).reshape(b, s, h, p)


def mamba2_branch(s_z, s_xbc, s_dt, conv_w, conv_b, dt_bias, a_log, d_skip, norm_w, w_out):
    b, s, _ = s_xbc.shape
    xbc = jax.nn.silu(causal_depthwise_conv(s_xbc, conv_w, conv_b))
    xs, bm, cm = jnp.split(xbc, [D_INNER, D_INNER + N_GROUPS * D_STATE], axis=-1)
    xs = xs.reshape(b, s, N_HEADS, HEADDIM).astype(jnp.float32)
    bm = bm.reshape(b, s, N_GROUPS, D_STATE).astype(jnp.float32)
    cm = cm.reshape(b, s, N_GROUPS, D_STATE).astype(jnp.float32)
    dt = jax.nn.softplus(s_dt.astype(jnp.float32) + dt_bias.astype(jnp.float32))
    a = -jnp.exp(a_log.astype(jnp.float32))
    y = ssd_chunked(xs, dt, a, bm, cm)
    y = y + d_skip.astype(jnp.float32)[:, None] * xs
    y = y.reshape(b, s, D_INNER) * jax.nn.silu(s_z.astype(jnp.float32))
    yg = y.reshape(b, s, N_GROUPS, D_INNER // N_GROUPS)
    yg = yg * lax.rsqrt(jnp.mean(yg * yg, axis=-1, keepdims=True) + NORM_EPS)
    y = yg.reshape(b, s, D_INNER).astype(s_z.dtype) * norm_w
    return y @ w_out


def setup_inputs(seed: int = 0) -> dict:
    key = jax.random.key(seed)
    ks = jax.random.split(key, 17)
    f32 = jnp.float32
    L = DEPTH

    def nrm(k, shape, scale):
        return jax.random.normal(k, shape, f32) * scale

    x = jax.random.normal(ks[0], (BATCH, SEQ, D_MODEL), f32)
    norm_w = 1.0 + nrm(ks[1], (L, D_MODEL), 0.02)
    w_in = nrm(ks[2], (L, D_MODEL, D_IN_PROJ), D_MODEL ** -0.5)
    conv_a_w = nrm(ks[3], (L, CONV_WIDTH, CONV_CH), CONV_WIDTH ** -0.5)
    conv_a_b = nrm(ks[4], (L, CONV_CH), 0.02)
    ln_a_w = 1.0 + nrm(ks[5], (L, CONV_CH), 0.02)
    ln_a_b = nrm(ks[6], (L, CONV_CH), 0.02)
    w_a_out = nrm(ks[7], (L, CONV_CH, D_MODEL), CONV_CH ** -0.5)
    conv_b_w = nrm(ks[8], (L, SSM_CONV, XBC_DIM), SSM_CONV ** -0.5)
    conv_b_b = nrm(ks[9], (L, XBC_DIM), 0.02)
    u = jax.random.uniform(ks[10], (L, N_HEADS), f32)
    dt0 = jnp.exp(u * (math.log(DT_MAX) - math.log(DT_MIN)) + math.log(DT_MIN))
    dt_bias = dt0 + jnp.log(-jnp.expm1(-dt0))
    a_log = jnp.log(jax.random.uniform(ks[11], (L, N_HEADS), f32, 1.0, 16.0))
    d_skip = 1.0 + nrm(ks[12], (L, N_HEADS), 0.1)
    ssm_norm_w = 1.0 + nrm(ks[13], (L, D_INNER), 0.02)
    w_b_out = nrm(ks[14], (L, D_INNER, D_MODEL), D_INNER ** -0.5)
    w_o = nrm(ks[15], (L, D_MODEL, D_MODEL), D_MODEL ** -0.5)
    final_norm_w = 1.0 + nrm(ks[16], (D_MODEL,), 0.02)
    return {"x": x, "norm_w": norm_w, "w_in": w_in,
            "conv_a_w": conv_a_w, "conv_a_b": conv_a_b, "ln_a_w": ln_a_w, "ln_a_b": ln_a_b,
            "w_a_out": w_a_out, "conv_b_w": conv_b_w, "conv_b_b": conv_b_b,
            "dt_bias": dt_bias, "a_log": a_log, "d_skip": d_skip, "ssm_norm_w": ssm_norm_w,
            "w_b_out": w_b_out, "w_o": w_o, "final_norm_w": final_norm_w}


def reference(x, norm_w, w_in, conv_a_w, conv_a_b, ln_a_w, ln_a_b, w_a_out,
              conv_b_w, conv_b_b, dt_bias, a_log, d_skip, ssm_norm_w, w_b_out, w_o,
              final_norm_w):
    for i in range(DEPTH):
        h = rmsnorm(x, norm_w[i])
        proj = h @ w_in[i]
        a_val, a_gate, a_z, s_z, s_xbc, s_dt, g_a, g_b = jnp.split(proj, SPLITS, axis=-1)
        y_a = conformer_conv_branch(a_val, a_gate, a_z, conv_a_w[i], conv_a_b[i],
                                    ln_a_w[i], ln_a_b[i], w_a_out[i])
        y_b = mamba2_branch(s_z, s_xbc, s_dt, conv_b_w[i], conv_b_b[i], dt_bias[i],
                            a_log[i], d_skip[i], ssm_norm_w[i], w_b_out[i])
        merged = jax.nn.sigmoid(g_a) * y_a + jax.nn.sigmoid(g_b) * y_b
        x = x + merged @ w_o[i]
    return rmsnorm(x, final_norm_w)
```

```python
import functools

import jax
import jax.numpy as jnp
from jax import lax
from jax.experimental import pallas as pl
from jax.experimental.pallas import tpu as pltpu

F32 = jnp.float32
BF16 = jnp.bfloat16

D_MODEL = 2048
CONV_CH = D_MODEL
CONV_WIDTH = 31
D_INNER = 2 * D_MODEL
HEADDIM = 64
N_HEADS = D_INNER // HEADDIM
D_STATE = 128
N_GROUPS = 8
HEADS_PER_GROUP = N_HEADS // N_GROUPS
GROUP_W = HEADS_PER_GROUP * HEADDIM
SSM_CONV = 4
CHUNK = 128
XBC_DIM = D_INNER + 2 * N_GROUPS * D_STATE
NORM_EPS = 1e-6
LN_EPS = 1e-5

OFF_A_VAL = 0
OFF_A_GATE = OFF_A_VAL + CONV_CH
OFF_A_Z = OFF_A_GATE + CONV_CH
OFF_S_Z = OFF_A_Z + CONV_CH
OFF_S_X = OFF_S_Z + D_INNER
OFF_S_B = OFF_S_X + D_INNER
OFF_S_C = OFF_S_B + N_GROUPS * D_STATE
OFF_S_DT = OFF_S_C + N_GROUPS * D_STATE
OFF_G = OFF_S_DT + N_HEADS

LANES = 128
CONV_HALO = 32
TAIL_ROWS = 8
VMEM_LIMIT = 56 * 1024 * 1024


def _sigmoid(v):
    return 1.0 / (1.0 + jnp.exp(-v))


def _silu(v):
    return v * _sigmoid(v)


def _softplus(v):
    return jnp.maximum(v, 0.0) + jnp.log1p(jnp.exp(-jnp.abs(v)))


def _split3(v):
    v0 = v.astype(BF16)
    r = v - v0.astype(F32)
    v1 = r.astype(BF16)
    r = r - v1.astype(F32)
    return v0, v1, r.astype(BF16)


def _dot(a, b):
    return jnp.dot(a, b, preferred_element_type=F32)


def _rmsnorm_kernel(x_ref, w_ref, o_ref):
    x = x_ref[...]
    ms = jnp.mean(x * x, axis=-1, keepdims=True)
    o_ref[...] = (x * lax.rsqrt(ms + NORM_EPS) * w_ref[...]).astype(o_ref.dtype)


def _rmsnorm_cast(x2, w, tm=512):
    m, d = x2.shape
    return pl.pallas_call(
        _rmsnorm_kernel,
        out_shape=jax.ShapeDtypeStruct((m, d), BF16),
        grid=(m // tm,),
        in_specs=[pl.BlockSpec((tm, d), lambda i: (i, 0)),
                  pl.BlockSpec((1, d), lambda i: (0, 0))],
        out_specs=pl.BlockSpec((tm, d), lambda i: (i, 0)),
        compiler_params=pltpu.CompilerParams(dimension_semantics=("arbitrary",),
                                             vmem_limit_bytes=VMEM_LIMIT),
        name="rmsnorm_cast",
    )(x2, w.reshape(1, d))


def _proj_kernel(*refs, mode, nsplit):
    if mode == "glu":
        h_ref, w_ref, w2_ref, o_ref = refs
    else:
        h_ref, w_ref, o_ref = refs
    h = h_ref[...]
    acc = _dot(h, w_ref[...])
    if mode == "glu":
        res = acc * _sigmoid(_dot(h, w2_ref[...]))
    elif mode == "silu":
        res = _silu(acc)
    elif mode == "sigmoid":
        res = _sigmoid(acc)
    else:
        res = acc
    width = res.shape[1] // nsplit
    for s in range(nsplit):
        o_ref[s] = res[:, s * width:(s + 1) * width].astype(o_ref.dtype)


def _proj(h, w, col0, ncols, mode, *, split_w=None, gate_col0=None, tm=1024, tn=512):
    m, k = h.shape
    cb0 = col0 // tn
    in_specs = [pl.BlockSpec((tm, k), lambda i, j: (i, 0)),
                pl.BlockSpec((k, tn), lambda i, j: (0, cb0 + j))]
    args = [h, w]
    if mode == "glu":
        gb0 = gate_col0 // tn
        in_specs.append(pl.BlockSpec((k, tn), lambda i, j: (0, gb0 + j)))
        args.append(w)
    if split_w is None:
        nsplit = 1
        out_shape = jax.ShapeDtypeStruct((1, m, ncols), BF16)
        out_spec = pl.BlockSpec((1, tm, tn), lambda i, j: (0, i, j))
    else:
        nsplit = tn // split_w
        out_shape = jax.ShapeDtypeStruct((ncols // split_w, m, split_w), BF16)
        out_spec = pl.BlockSpec((nsplit, tm, split_w), lambda i, j: (j, i, 0))
    out = pl.pallas_call(
        functools.partial(_proj_kernel, mode=mode, nsplit=nsplit),
        out_shape=out_shape,
        grid=(m // tm, ncols // tn),
        in_specs=in_specs,
        out_specs=out_spec,
        compiler_params=pltpu.CompilerParams(dimension_semantics=("arbitrary", "arbitrary"),
                                             vmem_limit_bytes=VMEM_LIMIT),
        name="proj_" + mode,
    )(*args)
    return out[0] if split_w is None else out


def _scalars_kernel(h_ref, w_ref, bias_ref, alog_ref,
                    cum_ref, ecum_ref, te_ref, cumT_ref, dtT_ref):
    tm = h_ref.shape[0]
    nh = w_ref.shape[1]
    dt = _softplus(_dot(h_ref[...], w_ref[...]) + bias_ref[...])
    da = dt * (-jnp.exp(alog_ref[...]))
    row = lax.broadcasted_iota(jnp.int32, (CHUNK, CHUNK), 0)
    col = lax.broadcasted_iota(jnp.int32, (CHUNK, CHUNK), 1)
    tril = (row >= col).astype(BF16)
    for c in range(tm // CHUNK):
        sl = slice(c * CHUNK, (c + 1) * CHUNK)
        d0, d1, d2 = _split3(da[sl])
        cum = _dot(tril, d0) + _dot(tril, d1) + _dot(tril, d2)
        dtc = dt[sl]
        cum_ref[sl, :] = cum
        ecum_ref[sl, :] = jnp.exp(cum)
        te_ref[sl, :] = jnp.exp(cum[CHUNK - 1:CHUNK, :] - cum) * dtc
        both_t = jnp.concatenate([cum, dtc], axis=1).T
        cumT_ref[:, sl] = both_t[:nh]
        dtT_ref[:, sl] = both_t[nh:]


def _ssd_scalars(h, w_dt, dt_bias, a_log, tm=1024):
    m, k = h.shape
    nh = w_dt.shape[1]
    tok = jax.ShapeDtypeStruct((m, nh), F32)
    tr = jax.ShapeDtypeStruct((nh, m), F32)
    tok_spec = pl.BlockSpec((tm, nh), lambda i: (i, 0))
    tr_spec = pl.BlockSpec((nh, tm), lambda i: (0, i))
    return pl.pallas_call(
        _scalars_kernel,
        out_shape=(tok, tok, tok, tr, tr),
        grid=(m // tm,),
        in_specs=[pl.BlockSpec((tm, k), lambda i: (i, 0)),
                  pl.BlockSpec((k, nh), lambda i: (0, 0)),
                  pl.BlockSpec((1, nh), lambda i: (0, 0)),
                  pl.BlockSpec((1, nh), lambda i: (0, 0))],
        out_specs=(tok_spec, tok_spec, tok_spec, tr_spec, tr_spec),
        compiler_params=pltpu.CompilerParams(dimension_semantics=("arbitrary",),
                                             vmem_limit_bytes=VMEM_LIMIT),
        name="ssd_scalars",
    )(h, w_dt, dt_bias.reshape(1, nh), a_log.reshape(1, nh))


def _conf_kernel(u_ref, za_ref, cw_ref, cb_ref, lnw_ref, lnb_ref, o_ref, ubuf, cbuf, *, rows, row_chunk):
    nblk = ubuf.shape[0]

    @pl.when(pl.program_id(1) == 0)
    def _():
        ubuf[:, 0:CONV_HALO, :] = jnp.zeros((nblk, CONV_HALO, LANES), F32)

    for j in range(nblk):
        ubuf[j, CONV_HALO:CONV_HALO + rows, :] = u_ref[:, j * LANES:(j + 1) * LANES].astype(F32)

    first = CONV_HALO - (CONV_WIDTH - 1)

    def conv_block(j, carry):
        w = cw_ref[j]
        for r0 in range(0, rows, row_chunk):
            acc = jnp.zeros((row_chunk, LANES), F32)
            for k in range(CONV_WIDTH):
                acc = acc + w[k:k + 1, :] * ubuf[j, pl.ds(r0 + first + k, row_chunk), :]
            cbuf[j, r0:r0 + row_chunk, :] = acc
        ubuf[j, 0:CONV_HALO, :] = ubuf[j, rows:rows + CONV_HALO, :]
        return carry

    lax.fori_loop(0, nblk, conv_block, 0)

    def ln_block(i, carry):
        r0 = pl.multiple_of(i * row_chunk, row_chunk)
        y = jnp.concatenate([cbuf[j, pl.ds(r0, row_chunk), :] for j in range(nblk)], axis=1) + cb_ref[...]
        mu = jnp.mean(y, axis=-1, keepdims=True)
        d = y - mu
        var = jnp.mean(d * d, axis=-1, keepdims=True)
        v = d * lax.rsqrt(var + LN_EPS) * lnw_ref[...] + lnb_ref[...]
        v = _silu(v) * za_ref[pl.ds(r0, row_chunk), :].astype(F32)
        o_ref[pl.ds(r0, row_chunk), :] = v.astype(o_ref.dtype)
        return carry

    lax.fori_loop(0, rows // row_chunk, ln_block, 0)


def _conformer_pre(u, za, conv_w, conv_b, ln_w, ln_b, batch, seq, rows=256, row_chunk=64):
    m, c = u.shape
    nblk = c // LANES
    taps = conv_w.shape[0]
    cw = jnp.pad(conv_w, ((0, CONV_HALO - taps), (0, 0))).reshape(CONV_HALO, nblk, LANES).transpose(1, 0, 2)
    nt = seq // rows
    tile = pl.BlockSpec((rows, c), lambda b, t: (b * nt + t, 0))
    vec = pl.BlockSpec((1, c), lambda b, t: (0, 0))
    return pl.pallas_call(
        functools.partial(_conf_kernel, rows=rows, row_chunk=row_chunk),
        out_shape=jax.ShapeDtypeStruct((m, c), BF16),
        grid=(batch, nt),
        in_specs=[tile, tile,
                  pl.BlockSpec((nblk, CONV_HALO, LANES), lambda b, t: (0, 0, 0)),
                  vec, vec, vec],
        out_specs=tile,
        scratch_shapes=[pltpu.VMEM((nblk, CONV_HALO + rows, LANES), F32),
                        pltpu.VMEM((nblk, rows, LANES), F32)],
        compiler_params=pltpu.CompilerParams(dimension_semantics=("arbitrary", "arbitrary"),
                                             vmem_limit_bytes=VMEM_LIMIT),
        name="conformer_pre",
    )(u, za, cw, conv_b.reshape(1, c), ln_w.reshape(1, c), ln_b.reshape(1, c))


def _short_conv_silu(cur_ref, tail_ref, w_ref, b_ref, g):
    cur = cur_ref[g].astype(F32)
    ext = jnp.concatenate([tail_ref[g], cur], axis=0)
    w = w_ref[g]
    acc = b_ref[g]
    for k in range(SSM_CONV):
        start = TAIL_ROWS - (SSM_CONV - 1) + k
        acc = acc + w[k:k + 1, :] * ext[start:start + CHUNK, :]
    tail_ref[g] = cur[CHUNK - TAIL_ROWS:, :]
    return _silu(acc)


def _ssd_kernel(x_ref, b_ref, c_ref, zs_ref, cum_ref, ecum_ref, te_ref, cumT_ref, dtT_ref,
                wx_ref, bx_ref, wb_ref, bb_ref, wc_ref, bc_ref, dsk_ref, nw_ref,
                o_ref,
                state_ref, tx_ref, tb_ref, tc_ref, cg_ref, eg_ref, tg_ref):
    L = CHUNK
    hpg = HEADS_PER_GROUP

    @pl.when(pl.program_id(1) == 0)
    def _():
        state_ref[...] = jnp.zeros(state_ref.shape, F32)
        tx_ref[...] = jnp.zeros(tx_ref.shape, F32)
        tb_ref[...] = jnp.zeros(tb_ref.shape, F32)
        tc_ref[...] = jnp.zeros(tc_ref.shape, F32)

    cum = cum_ref[...]
    ecum = ecum_ref[...]
    te = te_ref[...]
    for g in range(N_GROUPS):
        cg_ref[g] = cum[:, g * hpg:(g + 1) * hpg]
        eg_ref[g] = ecum[:, g * hpg:(g + 1) * hpg]
        tg_ref[g] = te[:, g * hpg:(g + 1) * hpg]

    row = lax.broadcasted_iota(jnp.int32, (L, L), 0)
    col = lax.broadcasted_iota(jnp.int32, (L, L), 1)
    causal = row >= col
    low_half = col < HEADDIM

    def group_body(g, carry):
        xs = _short_conv_silu(x_ref, tx_ref, wx_ref, bx_ref, g)
        bm = _short_conv_silu(b_ref, tb_ref, wb_ref, bb_ref, g)
        cm = _short_conv_silu(c_ref, tc_ref, wc_ref, bc_ref, g)
        bt = bm.T.astype(BF16)
        cb16 = cm.astype(BF16)
        cb = _dot(cb16, bt)
        st = state_ref[g]
        y_state = _dot(cb16, st.astype(BF16))
        c8 = cg_ref[g]
        e8 = eg_ref[g]
        t8 = tg_ref[g]
        g8 = pl.multiple_of(g * hpg, hpg)
        ct8 = cumT_ref[pl.ds(g8, hpg), :]
        dt8 = dtT_ref[pl.ds(g8, hpg), :]
        ys, xws, lasts = [], [], []
        for q in range(hpg // 2):
            xp = xs[:, q * LANES:(q + 1) * LANES]
            ms = []
            for e in range(2):
                r = 2 * q + e
                seg = jnp.broadcast_to(c8[:, r:r + 1], (L, L)) - ct8[r:r + 1, :]
                decay = jnp.exp(jnp.where(causal, seg, -jnp.inf))
                ms.append((cb * decay * dt8[r:r + 1, :]).astype(BF16))
            ecol = jnp.where(low_half, jnp.broadcast_to(e8[:, 2 * q:2 * q + 1], (L, LANES)),
                             jnp.broadcast_to(e8[:, 2 * q + 1:2 * q + 2], (L, LANES)))
            tcol = jnp.where(low_half, jnp.broadcast_to(t8[:, 2 * q:2 * q + 1], (L, LANES)),
                             jnp.broadcast_to(t8[:, 2 * q + 1:2 * q + 2], (L, LANES)))
            lhs = jnp.concatenate(ms, axis=1)
            rhs = jnp.concatenate([jnp.where(low_half, xp, 0.0), jnp.where(low_half, 0.0, xp)],
                                  axis=0).astype(BF16)
            ys.append(_dot(lhs, rhs) + y_state[:, q * LANES:(q + 1) * LANES] * ecol)
            xws.append(xp * tcol)
            lasts.append(ecol[L - 1:L, :])
        y = jnp.concatenate(ys, axis=1)
        xw = jnp.concatenate(xws, axis=1)
        state_ref[g] = st * jnp.concatenate(lasts, axis=1) + _dot(bt, xw.astype(BF16))
        y = y + dsk_ref[g] * xs
        y = y * zs_ref[g].astype(F32)
        y = y * lax.rsqrt(jnp.mean(y * y, axis=-1, keepdims=True) + NORM_EPS)
        o_ref[g] = (y * nw_ref[g]).astype(o_ref.dtype)
        return carry

    lax.fori_loop(0, N_GROUPS, group_body, 0)


def _ssd_pre(xs, bm, cm, zs, cum, ecum, te, cumT, dtT, conv_w, conv_b, d_skip, norm_w, batch, seq):
    g, m, gw = xs.shape
    n = bm.shape[2]
    nh = cum.shape[1]
    nc = seq // CHUNK

    def grp(width):
        return pl.BlockSpec((g, CHUNK, width), lambda b, c: (0, b * nc + c, 0))

    def const3(shape):
        return pl.BlockSpec(shape, lambda b, c: (0, 0, 0))

    tok = pl.BlockSpec((CHUNK, nh), lambda b, c: (b * nc + c, 0))
    tr = pl.BlockSpec((nh, CHUNK), lambda b, c: (0, b * nc + c))

    def group_major(v, width):
        return v.reshape(v.shape[0], g, width).transpose(1, 0, 2)

    wx = group_major(conv_w[:, :D_INNER], gw)
    wb = group_major(conv_w[:, D_INNER:D_INNER + g * n], n)
    wc = group_major(conv_w[:, D_INNER + g * n:], n)
    cbias = conv_b.reshape(1, -1)
    bx = group_major(cbias[:, :D_INNER], gw)
    bb = group_major(cbias[:, D_INNER:D_INNER + g * n], n)
    bc = group_major(cbias[:, D_INNER + g * n:], n)
    dsk = jnp.repeat(d_skip, HEADDIM).reshape(g, 1, gw)
    nw = norm_w.reshape(g, 1, gw)
    hpg = nh // g
    return pl.pallas_call(
        _ssd_kernel,
        out_shape=jax.ShapeDtypeStruct((g, m, gw), BF16),
        grid=(batch, nc),
        in_specs=[grp(gw), grp(n), grp(n), grp(gw), tok, tok, tok, tr, tr,
                  const3((g, SSM_CONV, gw)), const3((g, 1, gw)),
                  const3((g, SSM_CONV, n)), const3((g, 1, n)),
                  const3((g, SSM_CONV, n)), const3((g, 1, n)),
                  const3((g, 1, gw)), const3((g, 1, gw))],
        out_specs=grp(gw),
        scratch_shapes=[pltpu.VMEM((g, n, gw), F32),
                        pltpu.VMEM((g, TAIL_ROWS, gw), F32),
                        pltpu.VMEM((g, TAIL_ROWS, n), F32),
                        pltpu.VMEM((g, TAIL_ROWS, n), F32),
                        pltpu.VMEM((g, CHUNK, hpg), F32),
                        pltpu.VMEM((g, CHUNK, hpg), F32),
                        pltpu.VMEM((g, CHUNK, hpg), F32)],
        compiler_params=pltpu.CompilerParams(dimension_semantics=("arbitrary", "arbitrary"),
                                             vmem_limit_bytes=VMEM_LIMIT),
        name="ssd_pre",
    )(xs, bm, cm, zs, cum, ecum, te, cumT, dtT, wx, bx, wb, bb, wc, bc, dsk, nw)


def _merge_kernel(a_ref, y_ref, wa_ref, wb_ref, ga_ref, gb_ref, o_ref):
    acc_a = _dot(a_ref[...], wa_ref[...])
    acc_b = _dot(y_ref[0], wb_ref[0])
    for g in range(1, y_ref.shape[0]):
        acc_b = acc_b + _dot(y_ref[g], wb_ref[g])
    merged = ga_ref[...].astype(F32) * acc_a + gb_ref[...].astype(F32) * acc_b
    o_ref[...] = merged.astype(o_ref.dtype)


def _merge(act_a, y_b, w_a, w_b, gates, tm=512, tn=512):
    m, ka = act_a.shape
    g, _, gw = y_b.shape
    n = w_a.shape[1]
    gate_blocks = n // tn
    return pl.pallas_call(
        _merge_kernel,
        out_shape=jax.ShapeDtypeStruct((m, n), BF16),
        grid=(m // tm, n // tn),
        in_specs=[pl.BlockSpec((tm, ka), lambda i, j: (i, 0)),
                  pl.BlockSpec((g, tm, gw), lambda i, j: (0, i, 0)),
                  pl.BlockSpec((ka, tn), lambda i, j: (0, j)),
                  pl.BlockSpec((g, gw, tn), lambda i, j: (0, 0, j)),
                  pl.BlockSpec((tm, tn), lambda i, j: (i, j)),
                  pl.BlockSpec((tm, tn), lambda i, j: (i, gate_blocks + j))],
        out_specs=pl.BlockSpec((tm, tn), lambda i, j: (i, j)),
        compiler_params=pltpu.CompilerParams(dimension_semantics=("arbitrary", "arbitrary"),
                                             vmem_limit_bytes=VMEM_LIMIT),
        name="merge",
    )(act_a, y_b, w_a, w_b.reshape(g, gw, n), gates, gates)


def _final_kernel(m_ref, w_ref, x_ref, nw_ref, o_ref):
    x = x_ref[...] + _dot(m_ref[...], w_ref[...])
    ms = jnp.mean(x * x, axis=-1, keepdims=True)
    o_ref[...] = x * lax.rsqrt(ms + NORM_EPS) * nw_ref[...]


def _final(merged, w_o, x2, norm_w, tm=256):
    m, d = x2.shape
    return pl.pallas_call(
        _final_kernel,
        out_shape=jax.ShapeDtypeStruct((m, d), F32),
        grid=(m // tm,),
        in_specs=[pl.BlockSpec((tm, d), lambda i: (i, 0)),
                  pl.BlockSpec((d, d), lambda i: (0, 0)),
                  pl.BlockSpec((tm, d), lambda i: (i, 0)),
                  pl.BlockSpec((1, d), lambda i: (0, 0))],
        out_specs=pl.BlockSpec((tm, d), lambda i: (i, 0)),
        compiler_params=pltpu.CompilerParams(dimension_semantics=("arbitrary",),
                                             vmem_limit_bytes=VMEM_LIMIT),
        name="final_norm",
    )(merged, w_o, x2, norm_w.reshape(1, d))


def _layer(x2, batch, seq, norm_w, w_in, conv_a_w, conv_a_b, ln_a_w, ln_a_b, w_a_out,
           conv_b_w, conv_b_b, dt_bias, a_log, d_skip, ssm_norm_w, w_b_out, w_o):
    w_main = w_in[:, :OFF_S_DT].astype(BF16)
    w_dt = w_in[:, OFF_S_DT:OFF_G].astype(BF16)
    w_gate = w_in[:, OFF_G:].astype(BF16)

    h = _rmsnorm_cast(x2, norm_w)
    u = _proj(h, w_main, OFF_A_VAL, CONV_CH, "glu", gate_col0=OFF_A_GATE)
    za = _proj(h, w_main, OFF_A_Z, CONV_CH, "silu")
    zs = _proj(h, w_main, OFF_S_Z, D_INNER, "silu", split_w=GROUP_W)
    xs = _proj(h, w_main, OFF_S_X, D_INNER, "id", split_w=GROUP_W)
    bm = _proj(h, w_main, OFF_S_B, N_GROUPS * D_STATE, "id", split_w=D_STATE)
    cm = _proj(h, w_main, OFF_S_C, N_GROUPS * D_STATE, "id", split_w=D_STATE)
    gates = _proj(h, w_gate, 0, 2 * D_MODEL, "sigmoid")
    cum, ecum, te, cumT, dtT = _ssd_scalars(h, w_dt, dt_bias, a_log)

    act_a = _conformer_pre(u, za, conv_a_w, conv_a_b, ln_a_w, ln_a_b, batch, seq)
    y_b = _ssd_pre(xs, bm, cm, zs, cum, ecum, te, cumT, dtT, conv_b_w, conv_b_b, d_skip, ssm_norm_w,
                   batch, seq)
    return _merge(act_a, y_b, w_a_out.astype(BF16), w_b_out.astype(BF16), gates)


def kernel(x, norm_w, w_in, conv_a_w, conv_a_b, ln_a_w, ln_a_b, w_a_out, conv_b_w, conv_b_b, dt_bias, a_log,
           d_skip, ssm_norm_w, w_b_out, w_o, final_norm_w):
    batch, seq, d = x.shape
    depth = norm_w.shape[0]
    assert depth == 1, "the residual add is fused with the final norm for a single layer"
    x2 = x.reshape(batch * seq, d)
    merged = _layer(x2, batch, seq, norm_w[0], w_in[0], conv_a_w[0], conv_a_b[0], ln_a_w[0], ln_a_b[0],
                    w_a_out[0], conv_b_w[0], conv_b_b[0], dt_bias[0], a_log[0], d_skip[0], ssm_norm_w[0],
                    w_b_out[0], w_o[0])
    out = _final(merged, w_o[0].astype(BF16), x2, final_norm_w)
    return out.reshape(batch, seq, d)
```

```python
import functools
import math

import jax
import jax.numpy as jnp
from jax import lax
from jax.experimental import pallas as pl
from jax.experimental.pallas import tpu as pltpu

F32 = jnp.float32
BF16 = jnp.bfloat16

D_MODEL = 2048
CONV_CH = D_MODEL
CONV_WIDTH = 31
D_INNER = 2 * D_MODEL
HEADDIM = 64
N_HEADS = D_INNER // HEADDIM
D_STATE = 128
N_GROUPS = 8
HEADS_PER_GROUP = N_HEADS // N_GROUPS
GROUP_W = HEADS_PER_GROUP * HEADDIM
SSM_CONV = 4
CHUNK = 128
XBC_DIM = D_INNER + 2 * N_GROUPS * D_STATE
NORM_EPS = 1e-6
LN_EPS = 1e-5
LOG2E = math.log2(math.e)

OFF_A_VAL = 0
OFF_A_GATE = OFF_A_VAL + CONV_CH
OFF_A_Z = OFF_A_GATE + CONV_CH
OFF_S_Z = OFF_A_Z + CONV_CH
OFF_S_X = OFF_S_Z + D_INNER
OFF_S_B = OFF_S_X + D_INNER
OFF_S_C = OFF_S_B + N_GROUPS * D_STATE
OFF_S_DT = OFF_S_C + N_GROUPS * D_STATE
OFF_G = OFF_S_DT + N_HEADS

LANES = 128
CONV_HALO = 32
TAIL_ROWS = 8
VMEM_LIMIT = 56 * 1024 * 1024


def _sigmoid(v):
    return 1.0 / (1.0 + jnp.exp(-v))


def _silu(v):
    return v * _sigmoid(v)


def _softplus(v):
    return jnp.maximum(v, 0.0) + jnp.log1p(jnp.exp(-jnp.abs(v)))


def _split2(v):
    v0 = v.astype(BF16)
    return v0, (v - v0.astype(F32)).astype(BF16)


def _split3(v):
    v0 = v.astype(BF16)
    r = v - v0.astype(F32)
    v1 = r.astype(BF16)
    r = r - v1.astype(F32)
    return v0, v1, r.astype(BF16)


def _dot(a, b):
    return jnp.dot(a, b, preferred_element_type=F32)


def _rmsnorm_kernel(x_ref, w_ref, o_ref):
    x = x_ref[...]
    ms = jnp.mean(x * x, axis=-1, keepdims=True)
    o_ref[...] = (x * lax.rsqrt(ms + NORM_EPS) * w_ref[...]).astype(o_ref.dtype)


def _rmsnorm_cast(x2, w, tm=512):
    m, d = x2.shape
    return pl.pallas_call(
        _rmsnorm_kernel,
        out_shape=jax.ShapeDtypeStruct((m, d), BF16),
        grid=(m // tm,),
        in_specs=[pl.BlockSpec((tm, d), lambda i: (i, 0)),
                  pl.BlockSpec((1, d), lambda i: (0, 0))],
        out_specs=pl.BlockSpec((tm, d), lambda i: (i, 0)),
        compiler_params=pltpu.CompilerParams(dimension_semantics=("arbitrary",),
                                             vmem_limit_bytes=VMEM_LIMIT),
        name="rmsnorm_cast",
    )(x2, w.reshape(1, d))


def _proj_kernel(*refs, mode, nsplit):
    if mode == "glu":
        h_ref, w_ref, w2_ref, o_ref = refs
    else:
        h_ref, w_ref, o_ref = refs
    h = h_ref[...]
    acc = _dot(h, w_ref[...])
    if mode == "glu":
        res = acc * _sigmoid(_dot(h, w2_ref[...]))
    elif mode == "silu":
        res = _silu(acc)
    elif mode == "sigmoid":
        res = _sigmoid(acc)
    else:
        res = acc
    width = res.shape[1] // nsplit
    for s in range(nsplit):
        o_ref[s] = res[:, s * width:(s + 1) * width].astype(o_ref.dtype)


def _proj(h, w, col0, ncols, mode, *, split_w=None, gate_col0=None, tm=1024, tn=1024):
    m, k = h.shape
    cb0 = col0 // tn
    in_specs = [pl.BlockSpec((tm, k), lambda i, j: (i, 0)),
                pl.BlockSpec((k, tn), lambda i, j: (0, cb0 + j))]
    args = [h, w]
    if mode == "glu":
        gb0 = gate_col0 // tn
        in_specs.append(pl.BlockSpec((k, tn), lambda i, j: (0, gb0 + j)))
        args.append(w)
    if split_w is None:
        nsplit = 1
        out_shape = jax.ShapeDtypeStruct((1, m, ncols), BF16)
        out_spec = pl.BlockSpec((1, tm, tn), lambda i, j: (0, i, j))
    else:
        nsplit = tn // split_w
        out_shape = jax.ShapeDtypeStruct((ncols // split_w, m, split_w), BF16)
        out_spec = pl.BlockSpec((nsplit, tm, split_w), lambda i, j: (j, i, 0))
    out = pl.pallas_call(
        functools.partial(_proj_kernel, mode=mode, nsplit=nsplit),
        out_shape=out_shape,
        grid=(m // tm, ncols // tn),
        in_specs=in_specs,
        out_specs=out_spec,
        compiler_params=pltpu.CompilerParams(dimension_semantics=("arbitrary", "arbitrary"),
                                             vmem_limit_bytes=VMEM_LIMIT),
        name="proj_" + mode,
    )(*args)
    return out[0] if split_w is None else out


def _scalars_kernel(h_ref, w_ref, bias_ref, alog_ref, col_ref, et_ref, rowT_ref):
    tm = h_ref.shape[0]
    nh = w_ref.shape[1]
    dt = _softplus(_dot(h_ref[...], w_ref[...]) + bias_ref[...])
    da = dt * (-jnp.exp(alog_ref[...]))
    log2dt = jnp.log(dt) * LOG2E
    row = lax.broadcasted_iota(jnp.int32, (CHUNK, CHUNK), 0)
    col = lax.broadcasted_iota(jnp.int32, (CHUNK, CHUNK), 1)
    tril = (row >= col).astype(BF16)
    for c in range(tm // CHUNK):
        sl = slice(c * CHUNK, (c + 1) * CHUNK)
        d0, d1, d2 = _split3(da[sl])
        cum = _dot(tril, d0) + _dot(tril, d1) + _dot(tril, d2)
        cum2 = cum * LOG2E
        col_ref[sl, :] = cum2
        e_hi, e_lo = _split2(jnp.exp(cum))
        t_hi, t_lo = _split2(jnp.exp(cum[CHUNK - 1:CHUNK, :] - cum) * dt[sl])
        et_ref[sl, :] = jnp.concatenate([e_hi, t_hi, e_lo, t_lo], axis=1)
        both = jnp.concatenate([cum2 - log2dt[sl], cum2], axis=1)
        rowT_ref[:, sl] = both.T[:nh]


def _ssd_scalars(h, w_dt, dt_bias, a_log, tm=1024):
    m, k = h.shape
    nh = w_dt.shape[1]
    return pl.pallas_call(
        _scalars_kernel,
        out_shape=(jax.ShapeDtypeStruct((m, nh), F32),
                   jax.ShapeDtypeStruct((m, 4 * nh), BF16),
                   jax.ShapeDtypeStruct((nh, m), F32)),
        grid=(m // tm,),
        in_specs=[pl.BlockSpec((tm, k), lambda i: (i, 0)),
                  pl.BlockSpec((k, nh), lambda i: (0, 0)),
                  pl.BlockSpec((1, nh), lambda i: (0, 0)),
                  pl.BlockSpec((1, nh), lambda i: (0, 0))],
        out_specs=(pl.BlockSpec((tm, nh), lambda i: (i, 0)),
                   pl.BlockSpec((tm, 4 * nh), lambda i: (i, 0)),
                   pl.BlockSpec((nh, tm), lambda i: (0, i))),
        compiler_params=pltpu.CompilerParams(dimension_semantics=("arbitrary",),
                                             vmem_limit_bytes=VMEM_LIMIT),
        name="ssd_scalars",
    )(h, w_dt, dt_bias.reshape(1, nh), a_log.reshape(1, nh))


def _conf_kernel(u_ref, za_ref, cw_ref, lnw_ref, lnb_ref, o_ref, ubuf, cbuf, *, rows, conv_rows, ln_rows):
    nblk = ubuf.shape[0]

    @pl.when(pl.program_id(1) == 0)
    def _():
        ubuf[:, 0:CONV_HALO, :] = jnp.zeros((nblk, CONV_HALO, LANES), F32)

    for j in range(nblk):
        ubuf[j, CONV_HALO:CONV_HALO + rows, :] = u_ref[:, j * LANES:(j + 1) * LANES].astype(F32)

    first = CONV_HALO - (CONV_WIDTH - 1)

    def conv_block(j, carry):
        w = cw_ref[j]
        for r0 in range(0, rows, conv_rows):
            acc = jnp.broadcast_to(w[CONV_WIDTH:CONV_WIDTH + 1, :], (conv_rows, LANES))
            for k in range(CONV_WIDTH):
                acc = acc + w[k:k + 1, :] * ubuf[j, pl.ds(r0 + first + k, conv_rows), :]
            cbuf[j, r0:r0 + conv_rows, :] = acc
        ubuf[j, 0:CONV_HALO, :] = ubuf[j, rows:rows + CONV_HALO, :]
        return carry

    lax.fori_loop(0, nblk, conv_block, 0)

    inv_c = 1.0 / (nblk * LANES)

    def ln_block(i, carry):
        r0 = pl.multiple_of(i * ln_rows, ln_rows)
        rows_i = pl.ds(r0, ln_rows)
        total = cbuf[0, rows_i, :]
        for j in range(1, nblk):
            total = total + cbuf[j, rows_i, :]
        mu = jnp.sum(total, axis=-1, keepdims=True) * inv_c
        sq = jnp.zeros((ln_rows, LANES), F32)
        for j in range(nblk):
            d = cbuf[j, rows_i, :] - mu
            sq = sq + d * d
        rstd = lax.rsqrt(jnp.sum(sq, axis=-1, keepdims=True) * inv_c + LN_EPS)
        for j in range(nblk):
            lanes = slice(j * LANES, (j + 1) * LANES)
            v = (cbuf[j, rows_i, :] - mu) * rstd * lnw_ref[:, lanes] + lnb_ref[:, lanes]
            v = _silu(v) * za_ref[rows_i, lanes].astype(F32)
            o_ref[rows_i, lanes] = v.astype(o_ref.dtype)
        return carry

    lax.fori_loop(0, rows // ln_rows, ln_block, 0, unroll=2)


def _conformer_pre(u, za, conv_w, conv_b, ln_w, ln_b, batch, seq, rows=256, conv_rows=64, ln_rows=64):
    m, c = u.shape
    nblk = c // LANES
    assert conv_w.shape[0] == CONV_WIDTH and CONV_WIDTH + 1 <= CONV_HALO
    cw = jnp.concatenate([conv_w, conv_b.reshape(1, c)], axis=0)
    cw = jnp.pad(cw, ((0, CONV_HALO - CONV_WIDTH - 1), (0, 0))).reshape(CONV_HALO, nblk, LANES).transpose(1, 0, 2)
    nt = seq // rows
    tile = pl.BlockSpec((rows, c), lambda b, t: (b * nt + t, 0))
    vec = pl.BlockSpec((1, c), lambda b, t: (0, 0))
    return pl.pallas_call(
        functools.partial(_conf_kernel, rows=rows, conv_rows=conv_rows, ln_rows=ln_rows),
        out_shape=jax.ShapeDtypeStruct((m, c), BF16),
        grid=(batch, nt),
        in_specs=[tile, tile,
                  pl.BlockSpec((nblk, CONV_HALO, LANES), lambda b, t: (0, 0, 0)),
                  vec, vec],
        out_specs=tile,
        scratch_shapes=[pltpu.VMEM((nblk, CONV_HALO + rows, LANES), F32),
                        pltpu.VMEM((nblk, rows, LANES), F32)],
        compiler_params=pltpu.CompilerParams(dimension_semantics=("arbitrary", "arbitrary"),
                                             vmem_limit_bytes=VMEM_LIMIT),
        name="conformer_pre",
    )(u, za, cw, ln_w.reshape(1, c), ln_b.reshape(1, c))


def _short_conv_silu(buf_ref, w_ref, b_ref, g):
    w = w_ref[g]
    bias = b_ref[g]
    cols = []
    for c in range(buf_ref.shape[1]):
        lanes = slice(c * LANES, (c + 1) * LANES)
        acc = jnp.zeros((CHUNK, LANES), F32)
        for k in range(SSM_CONV):
            start = TAIL_ROWS - (SSM_CONV - 1) + k
            acc = acc + w[k:k + 1, lanes] * buf_ref[g, c, pl.ds(start, CHUNK), :]
        cols.append(acc + bias[:, lanes])
        buf_ref[g, c, 0:TAIL_ROWS, :] = buf_ref[g, c, CHUNK:CHUNK + TAIL_ROWS, :]
    return _silu(jnp.concatenate(cols, axis=1))


def _ssd_kernel(x_ref, b_ref, c_ref, zs_ref, col_ref, et_ref, rowT_ref, expand_ref,
                wx_ref, bx_ref, wb_ref, bb_ref, wc_ref, bc_ref, dsk_ref, nw_ref,
                o_ref,
                state_ref, xbuf, bbuf, cbuf, colg_ref):
    L = CHUNK
    hpg = HEADS_PER_GROUP

    @pl.when(pl.program_id(1) == 0)
    def _():
        state_ref[...] = jnp.zeros(state_ref.shape, F32)
        for buf in (xbuf, bbuf, cbuf):
            buf[:, :, 0:TAIL_ROWS, :] = jnp.zeros(buf.shape[:2] + (TAIL_ROWS, LANES), F32)

    colv = col_ref[...]
    for g in range(N_GROUPS):
        colg_ref[g] = colv[:, g * hpg:(g + 1) * hpg]
        for src, buf in ((x_ref, xbuf), (b_ref, bbuf), (c_ref, cbuf)):
            for c in range(buf.shape[1]):
                buf[g, c, TAIL_ROWS:TAIL_ROWS + L, :] = src[g, :, c * LANES:(c + 1) * LANES].astype(F32)

    row = lax.broadcasted_iota(jnp.int32, (L, L), 0)
    col = lax.broadcasted_iota(jnp.int32, (L, L), 1)
    causal = row >= col
    low_half = col < HEADDIM

    def group_body(g, carry):
        xs = _short_conv_silu(xbuf, wx_ref, bx_ref, g)
        bm = _short_conv_silu(bbuf, wb_ref, bb_ref, g)
        cm = _short_conv_silu(cbuf, wc_ref, bc_ref, g)
        bt = bm.T.astype(BF16)
        cm16 = cm.astype(BF16)
        cb16 = _dot(cm16, bt).astype(BF16)
        st = state_ref[g]
        y_state = _dot(cm16, st.astype(BF16))
        et = _dot(et_ref[...], expand_ref[g])
        ecol = et[:, :GROUP_W]
        tcol = et[:, GROUP_W:]
        c8 = colg_ref[g]
        r8 = rowT_ref[pl.ds(pl.multiple_of(g * hpg, hpg), hpg), :]
        ys = []
        for q in range(hpg // 2):
            xp = xs[:, q * LANES:(q + 1) * LANES]
            ms = []
            for e in range(2):
                r = 2 * q + e
                seg = jnp.broadcast_to(c8[:, r:r + 1], (L, L)) - r8[r:r + 1, :]
                decay = jnp.exp2(jnp.where(causal, seg, -jnp.inf))
                ms.append(cb16 * decay.astype(BF16))
            lhs = jnp.concatenate(ms, axis=1)
            rhs = jnp.concatenate([jnp.where(low_half, xp, 0.0), jnp.where(low_half, 0.0, xp)],
                                  axis=0).astype(BF16)
            ys.append(_dot(lhs, rhs))
        y = jnp.concatenate(ys, axis=1) + y_state * ecol
        state_ref[g] = st * ecol[L - 1:L, :] + _dot(bt, (xs * tcol).astype(BF16))
        y = y + dsk_ref[g] * xs
        y = y * zs_ref[g].astype(F32)
        y = y * lax.rsqrt(jnp.mean(y * y, axis=-1, keepdims=True) + NORM_EPS)
        o_ref[g] = (y * nw_ref[g]).astype(o_ref.dtype)
        return carry

    lax.fori_loop(0, N_GROUPS, group_body, 0, unroll=True)


def _head_expand_matrix():
    src = jnp.arange(4 * N_HEADS)
    head, is_t = src % N_HEADS, (src // N_HEADS) % 2
    dst = jnp.arange(2 * GROUP_W)
    dst_t, dst_head_in_group = dst // GROUP_W, (dst % GROUP_W) // HEADDIM
    g = jnp.arange(N_GROUPS)[:, None, None]
    hit = (head[None, :, None] == g * HEADS_PER_GROUP + dst_head_in_group[None, None, :]) \
        & (is_t[None, :, None] == dst_t[None, None, :])
    return hit.astype(BF16)


def _ssd_pre(xs, bm, cm, zs, colv, et, rowT, conv_w, conv_b, d_skip, norm_w, batch, seq):
    g, m, gw = xs.shape
    n = bm.shape[2]
    nh = colv.shape[1]
    nc = seq // CHUNK

    def grp(width):
        return pl.BlockSpec((g, CHUNK, width), lambda b, c: (0, b * nc + c, 0))

    def const3(shape):
        return pl.BlockSpec(shape, lambda b, c: (0, 0, 0))

    def group_major(v, width):
        return v.reshape(v.shape[0], g, width).transpose(1, 0, 2)

    wx = group_major(conv_w[:, :D_INNER], gw)
    wb = group_major(conv_w[:, D_INNER:D_INNER + g * n], n)
    wc = group_major(conv_w[:, D_INNER + g * n:], n)
    cbias = conv_b.reshape(1, -1)
    bx = group_major(cbias[:, :D_INNER], gw)
    bb = group_major(cbias[:, D_INNER:D_INNER + g * n], n)
    bc = group_major(cbias[:, D_INNER + g * n:], n)
    dsk = jnp.repeat(d_skip, HEADDIM).reshape(g, 1, gw)
    nw = norm_w.reshape(g, 1, gw)
    hpg = nh // g
    return pl.pallas_call(
        _ssd_kernel,
        out_shape=jax.ShapeDtypeStruct((g, m, gw), BF16),
        grid=(batch, nc),
        in_specs=[grp(gw), grp(n), grp(n), grp(gw),
                  pl.BlockSpec((CHUNK, nh), lambda b, c: (b * nc + c, 0)),
                  pl.BlockSpec((CHUNK, 4 * nh), lambda b, c: (b * nc + c, 0)),
                  pl.BlockSpec((nh, CHUNK), lambda b, c: (0, b * nc + c)),
                  const3((g, 4 * nh, 2 * gw)),
                  const3((g, SSM_CONV, gw)), const3((g, 1, gw)),
                  const3((g, SSM_CONV, n)), const3((g, 1, n)),
                  const3((g, SSM_CONV, n)), const3((g, 1, n)),
                  const3((g, 1, gw)), const3((g, 1, gw))],
        out_specs=grp(gw),
        scratch_shapes=[pltpu.VMEM((g, n, gw), F32),
                        pltpu.VMEM((g, gw // LANES, TAIL_ROWS + CHUNK, LANES), F32),
                        pltpu.VMEM((g, n // LANES, TAIL_ROWS + CHUNK, LANES), F32),
                        pltpu.VMEM((g, n // LANES, TAIL_ROWS + CHUNK, LANES), F32),
                        pltpu.VMEM((g, CHUNK, hpg), F32)],
        compiler_params=pltpu.CompilerParams(dimension_semantics=("arbitrary", "arbitrary"),
                                             vmem_limit_bytes=VMEM_LIMIT),
        name="ssd_pre",
    )(xs, bm, cm, zs, colv, et, rowT, _head_expand_matrix(), wx, bx, wb, bb, wc, bc, dsk, nw)


def _merge_kernel(a_ref, y_ref, wa_ref, wb_ref, ga_ref, gb_ref, o_ref):
    acc_a = _dot(a_ref[...], wa_ref[...])
    acc_b = _dot(y_ref[0], wb_ref[0])
    for g in range(1, y_ref.shape[0]):
        acc_b = acc_b + _dot(y_ref[g], wb_ref[g])
    merged = ga_ref[...].astype(F32) * acc_a + gb_ref[...].astype(F32) * acc_b
    o_ref[...] = merged.astype(o_ref.dtype)


def _merge(act_a, y_b, w_a, w_b, gates, tm=512, tn=512):
    m, ka = act_a.shape
    g, _, gw = y_b.shape
    n = w_a.shape[1]
    gate_blocks = n // tn
    return pl.pallas_call(
        _merge_kernel,
        out_shape=jax.ShapeDtypeStruct((m, n), BF16),
        grid=(m // tm, n // tn),
        in_specs=[pl.BlockSpec((tm, ka), lambda i, j: (i, 0)),
                  pl.BlockSpec((g, tm, gw), lambda i, j: (0, i, 0)),
                  pl.BlockSpec((ka, tn), lambda i, j: (0, j)),
                  pl.BlockSpec((g, gw, tn), lambda i, j: (0, 0, j)),
                  pl.BlockSpec((tm, tn), lambda i, j: (i, j)),
                  pl.BlockSpec((tm, tn), lambda i, j: (i, gate_blocks + j))],
        out_specs=pl.BlockSpec((tm, tn), lambda i, j: (i, j)),
        compiler_params=pltpu.CompilerParams(dimension_semantics=("arbitrary", "arbitrary"),
                                             vmem_limit_bytes=VMEM_LIMIT),
        name="merge",
    )(act_a, y_b, w_a, w_b.reshape(g, gw, n), gates, gates)


def _final_kernel(m_ref, w_ref, x_ref, nw_ref, o_ref):
    x = x_ref[...] + _dot(m_ref[...], w_ref[...])
    ms = jnp.mean(x * x, axis=-1, keepdims=True)
    o_ref[...] = x * lax.rsqrt(ms + NORM_EPS) * nw_ref[...]


def _final(merged, w_o, x2, norm_w, tm=256):
    m, d = x2.shape
    return pl.pallas_call(
        _final_kernel,
        out_shape=jax.ShapeDtypeStruct((m, d), F32),
        grid=(m // tm,),
        in_specs=[pl.BlockSpec((tm, d), lambda i: (i, 0)),
                  pl.BlockSpec((d, d), lambda i: (0, 0)),
                  pl.BlockSpec((tm, d), lambda i: (i, 0)),
                  pl.BlockSpec((1, d), lambda i: (0, 0))],
        out_specs=pl.BlockSpec((tm, d), lambda i: (i, 0)),
        compiler_params=pltpu.CompilerParams(dimension_semantics=("arbitrary",),
                                             vmem_limit_bytes=VMEM_LIMIT),
        name="final_norm",
    )(merged, w_o, x2, norm_w.reshape(1, d))


def _layer(x2, batch, seq, norm_w, w_in, conv_a_w, conv_a_b, ln_a_w, ln_a_b, w_a_out,
           conv_b_w, conv_b_b, dt_bias, a_log, d_skip, ssm_norm_w, w_b_out, w_o):
    w_all = w_in.astype(BF16)
    w_dt = w_in[:, OFF_S_DT:OFF_G].astype(BF16)
    w_gate = w_in[:, OFF_G:].astype(BF16)

    h = _rmsnorm_cast(x2, norm_w)
    u = _proj(h, w_all, OFF_A_VAL, CONV_CH, "glu", gate_col0=OFF_A_GATE, tn=512)
    za = _proj(h, w_all, OFF_A_Z, CONV_CH, "silu")
    zs = _proj(h, w_all, OFF_S_Z, D_INNER, "silu", split_w=GROUP_W)
    xs = _proj(h, w_all, OFF_S_X, D_INNER, "id", split_w=GROUP_W)
    bm = _proj(h, w_all, OFF_S_B, N_GROUPS * D_STATE, "id", split_w=D_STATE)
    cm = _proj(h, w_all, OFF_S_C, N_GROUPS * D_STATE, "id", split_w=D_STATE)
    gates = _proj(h, w_gate, 0, 2 * D_MODEL, "sigmoid")
    colv, et, rowT = _ssd_scalars(h, w_dt, dt_bias, a_log)

    act_a = _conformer_pre(u, za, conv_a_w, conv_a_b, ln_a_w, ln_a_b, batch, seq)
    y_b = _ssd_pre(xs, bm, cm, zs, colv, et, rowT, conv_b_w, conv_b_b, d_skip, ssm_norm_w, batch, seq)
    return _merge(act_a, y_b, w_a_out.astype(BF16), w_b_out.astype(BF16), gates)


def kernel(x, norm_w, w_in, conv_a_w, conv_a_b, ln_a_w, ln_a_b, w_a_out, conv_b_w, conv_b_b, dt_bias, a_log,
           d_skip, ssm_norm_w, w_b_out, w_o, final_norm_w):
    batch, seq, d = x.shape
    depth = norm_w.shape[0]
    assert depth == 1, "the residual add is fused with the final norm for a single layer"
    x2 = x.reshape(batch * seq, d)
    merged = _layer(x2, batch, seq, norm_w[0], w_in[0], conv_a_w[0], conv_a_b[0], ln_a_w[0], ln_a_b[0],
                    w_a_out[0], conv_b_w[0], conv_b_b[0], dt_bias[0], a_log[0], d_skip[0], ssm_norm_w[0],
                    w_b_out[0], w_o[0])
    out = _final(merged, w_o[0].astype(BF16), x2, final_norm_w)
    return out.reshape(batch, seq, d)
```

```python
import functools
import math

import jax
import jax.numpy as jnp
from jax import lax
from jax.experimental import pallas as pl
from jax.experimental.pallas import tpu as pltpu

F32 = jnp.float32
BF16 = jnp.bfloat16

D_MODEL = 2048
CONV_CH = D_MODEL
CONV_WIDTH = 31
D_INNER = 2 * D_MODEL
HEADDIM = 64
N_HEADS = D_INNER // HEADDIM
D_STATE = 128
N_GROUPS = 8
HEADS_PER_GROUP = N_HEADS // N_GROUPS
GROUP_W = HEADS_PER_GROUP * HEADDIM
SSM_CONV = 4
CHUNK = 128
XBC_DIM = D_INNER + 2 * N_GROUPS * D_STATE
NORM_EPS = 1e-6
LN_EPS = 1e-5
LOG2E = math.log2(math.e)

OFF_A_VAL = 0
OFF_A_GATE = OFF_A_VAL + CONV_CH
OFF_A_Z = OFF_A_GATE + CONV_CH
OFF_S_Z = OFF_A_Z + CONV_CH
OFF_S_X = OFF_S_Z + D_INNER
OFF_S_B = OFF_S_X + D_INNER
OFF_S_C = OFF_S_B + N_GROUPS * D_STATE
OFF_S_DT = OFF_S_C + N_GROUPS * D_STATE
OFF_G = OFF_S_DT + N_HEADS

LANES = 128
CONV_HALO = 32
TAIL_ROWS = 8
VMEM_LIMIT = 56 * 1024 * 1024


def _sigmoid(v):
    return 1.0 / (1.0 + jnp.exp(-v))


def _silu(v):
    return v * _sigmoid(v)


def _softplus(v):
    return jnp.maximum(v, 0.0) + jnp.log1p(jnp.exp(-jnp.abs(v)))


def _split2(v):
    v0 = v.astype(BF16)
    return v0, (v - v0.astype(F32)).astype(BF16)


def _split3(v):
    v0 = v.astype(BF16)
    r = v - v0.astype(F32)
    v1 = r.astype(BF16)
    r = r - v1.astype(F32)
    return v0, v1, r.astype(BF16)


def _dot(a, b):
    return jnp.dot(a, b, preferred_element_type=F32)


def _rmsnorm_kernel(x_ref, w_ref, o_ref):
    x = x_ref[...]
    ms = jnp.mean(x * x, axis=-1, keepdims=True)
    o_ref[...] = (x * lax.rsqrt(ms + NORM_EPS) * w_ref[...]).astype(o_ref.dtype)


def _rmsnorm_cast(x2, w, tm=512):
    m, d = x2.shape
    return pl.pallas_call(
        _rmsnorm_kernel,
        out_shape=jax.ShapeDtypeStruct((m, d), BF16),
        grid=(m // tm,),
        in_specs=[pl.BlockSpec((tm, d), lambda i: (i, 0)),
                  pl.BlockSpec((1, d), lambda i: (0, 0))],
        out_specs=pl.BlockSpec((tm, d), lambda i: (i, 0)),
        compiler_params=pltpu.CompilerParams(dimension_semantics=("arbitrary",),
                                             vmem_limit_bytes=VMEM_LIMIT),
        name="rmsnorm_cast",
    )(x2, w.reshape(1, d))


def _proj_kernel(*refs, mode, nsplit):
    if mode == "glu":
        h_ref, w_ref, w2_ref, o_ref, w16_ref, w216_ref = refs
    else:
        h_ref, w_ref, o_ref, w16_ref = refs

    @pl.when(pl.program_id(1) == 0)
    def _():
        w16_ref[...] = w_ref[...].astype(BF16)
        if mode == "glu":
            w216_ref[...] = w2_ref[...].astype(BF16)

    h = h_ref[...]
    acc = _dot(h, w16_ref[...])
    if mode == "glu":
        res = acc * _sigmoid(_dot(h, w216_ref[...]))
    elif mode == "silu":
        res = _silu(acc)
    elif mode == "sigmoid":
        res = _sigmoid(acc)
    else:
        res = acc
    width = res.shape[1] // nsplit
    for s in range(nsplit):
        o_ref[s] = res[:, s * width:(s + 1) * width].astype(o_ref.dtype)


def _proj(h, w, col0, ncols, mode, *, split_w=None, gate_col0=None, tm=1024, tn=1024):
    m, k = h.shape
    cb0 = col0 // tn
    in_specs = [pl.BlockSpec((tm, k), lambda j, i: (i, 0)),
                pl.BlockSpec((k, tn), lambda j, i: (0, cb0 + j))]
    args = [h, w]
    scratch = [pltpu.VMEM((k, tn), BF16)]
    if mode == "glu":
        gb0 = gate_col0 // tn
        in_specs.append(pl.BlockSpec((k, tn), lambda j, i: (0, gb0 + j)))
        args.append(w)
        scratch.append(pltpu.VMEM((k, tn), BF16))
    if split_w is None:
        nsplit = 1
        out_shape = jax.ShapeDtypeStruct((1, m, ncols), BF16)
        out_spec = pl.BlockSpec((1, tm, tn), lambda j, i: (0, i, j))
    else:
        nsplit = tn // split_w
        out_shape = jax.ShapeDtypeStruct((ncols // split_w, m, split_w), BF16)
        out_spec = pl.BlockSpec((nsplit, tm, split_w), lambda j, i: (j, i, 0))
    out = pl.pallas_call(
        functools.partial(_proj_kernel, mode=mode, nsplit=nsplit),
        out_shape=out_shape,
        grid=(ncols // tn, m // tm),
        in_specs=in_specs,
        out_specs=out_spec,
        scratch_shapes=scratch,
        compiler_params=pltpu.CompilerParams(dimension_semantics=("arbitrary", "arbitrary"),
                                             vmem_limit_bytes=VMEM_LIMIT),
        name="proj_" + mode,
    )(*args)
    return out[0] if split_w is None else out


def _scalars_kernel(h_ref, w_ref, bias_ref, alog_ref, col_ref, et_ref, rowT_ref):
    tm = h_ref.shape[0]
    nh = w_ref.shape[1]
    dt = _softplus(_dot(h_ref[...], w_ref[...]) + bias_ref[...])
    da = dt * (-jnp.exp(alog_ref[...]))
    log2dt = jnp.log(dt) * LOG2E
    row = lax.broadcasted_iota(jnp.int32, (CHUNK, CHUNK), 0)
    col = lax.broadcasted_iota(jnp.int32, (CHUNK, CHUNK), 1)
    tril = (row >= col).astype(BF16)
    for c in range(tm // CHUNK):
        sl = slice(c * CHUNK, (c + 1) * CHUNK)
        d0, d1, d2 = _split3(da[sl])
        cum = _dot(tril, d0) + _dot(tril, d1) + _dot(tril, d2)
        cum2 = cum * LOG2E
        col_ref[sl, :] = cum2
        e_hi, e_lo = _split2(jnp.exp(cum))
        t_hi, t_lo = _split2(jnp.exp(cum[CHUNK - 1:CHUNK, :] - cum) * dt[sl])
        et_ref[sl, :] = jnp.concatenate([e_hi, t_hi, e_lo, t_lo], axis=1)
        both = jnp.concatenate([cum2 - log2dt[sl], cum2], axis=1)
        rowT_ref[:, sl] = both.T[:nh]


def _ssd_scalars(h, w_dt, dt_bias, a_log, tm=1024):
    m, k = h.shape
    nh = w_dt.shape[1]
    return pl.pallas_call(
        _scalars_kernel,
        out_shape=(jax.ShapeDtypeStruct((m, nh), F32),
                   jax.ShapeDtypeStruct((m, 4 * nh), BF16),
                   jax.ShapeDtypeStruct((nh, m), F32)),
        grid=(m // tm,),
        in_specs=[pl.BlockSpec((tm, k), lambda i: (i, 0)),
                  pl.BlockSpec((k, nh), lambda i: (0, 0)),
                  pl.BlockSpec((1, nh), lambda i: (0, 0)),
                  pl.BlockSpec((1, nh), lambda i: (0, 0))],
        out_specs=(pl.BlockSpec((tm, nh), lambda i: (i, 0)),
                   pl.BlockSpec((tm, 4 * nh), lambda i: (i, 0)),
                   pl.BlockSpec((nh, tm), lambda i: (0, i))),
        compiler_params=pltpu.CompilerParams(dimension_semantics=("arbitrary",),
                                             vmem_limit_bytes=VMEM_LIMIT),
        name="ssd_scalars",
    )(h, w_dt, dt_bias.reshape(1, nh), a_log.reshape(1, nh))


def _conf_kernel(u_ref, za_ref, cw_ref, lnw_ref, lnb_ref, o_ref, ubuf, cbuf, *, rows, conv_rows, ln_rows):
    nblk = ubuf.shape[0]

    @pl.when(pl.program_id(1) == 0)
    def _():
        ubuf[:, 0:CONV_HALO, :] = jnp.zeros((nblk, CONV_HALO, LANES), F32)

    for j in range(nblk):
        ubuf[j, CONV_HALO:CONV_HALO + rows, :] = u_ref[:, j * LANES:(j + 1) * LANES].astype(F32)

    first = CONV_HALO - (CONV_WIDTH - 1)

    def conv_block(j, carry):
        w = cw_ref[j]
        for r0 in range(0, rows, conv_rows):
            acc = jnp.broadcast_to(w[CONV_WIDTH:CONV_WIDTH + 1, :], (conv_rows, LANES))
            for k in range(CONV_WIDTH):
                acc = acc + w[k:k + 1, :] * ubuf[j, pl.ds(r0 + first + k, conv_rows), :]
            cbuf[j, r0:r0 + conv_rows, :] = acc
        ubuf[j, 0:CONV_HALO, :] = ubuf[j, rows:rows + CONV_HALO, :]
        return carry

    lax.fori_loop(0, nblk, conv_block, 0)

    inv_c = 1.0 / (nblk * LANES)

    def ln_block(i, carry):
        r0 = pl.multiple_of(i * ln_rows, ln_rows)
        rows_i = pl.ds(r0, ln_rows)
        total = cbuf[0, rows_i, :]
        for j in range(1, nblk):
            total = total + cbuf[j, rows_i, :]
        mu = jnp.sum(total, axis=-1, keepdims=True) * inv_c
        sq = jnp.zeros((ln_rows, LANES), F32)
        for j in range(nblk):
            d = cbuf[j, rows_i, :] - mu
            sq = sq + d * d
        rstd = lax.rsqrt(jnp.sum(sq, axis=-1, keepdims=True) * inv_c + LN_EPS)
        for j in range(nblk):
            lanes = slice(j * LANES, (j + 1) * LANES)
            v = (cbuf[j, rows_i, :] - mu) * rstd * lnw_ref[:, lanes] + lnb_ref[:, lanes]
            v = _silu(v) * za_ref[rows_i, lanes].astype(F32)
            o_ref[rows_i, lanes] = v.astype(o_ref.dtype)
        return carry

    lax.fori_loop(0, rows // ln_rows, ln_block, 0, unroll=2)


def _conformer_pre(u, za, conv_w, conv_b, ln_w, ln_b, batch, seq, rows=256, conv_rows=64, ln_rows=64):
    m, c = u.shape
    nblk = c // LANES
    assert conv_w.shape[0] == CONV_WIDTH and CONV_WIDTH + 1 <= CONV_HALO
    cw = jnp.concatenate([conv_w, conv_b.reshape(1, c)], axis=0)
    cw = jnp.pad(cw, ((0, CONV_HALO - CONV_WIDTH - 1), (0, 0))).reshape(CONV_HALO, nblk, LANES).transpose(1, 0, 2)
    nt = seq // rows
    tile = pl.BlockSpec((rows, c), lambda b, t: (b * nt + t, 0))
    vec = pl.BlockSpec((1, c), lambda b, t: (0, 0))
    return pl.pallas_call(
        functools.partial(_conf_kernel, rows=rows, conv_rows=conv_rows, ln_rows=ln_rows),
        out_shape=jax.ShapeDtypeStruct((m, c), BF16),
        grid=(batch, nt),
        in_specs=[tile, tile,
                  pl.BlockSpec((nblk, CONV_HALO, LANES), lambda b, t: (0, 0, 0)),
                  vec, vec],
        out_specs=tile,
        scratch_shapes=[pltpu.VMEM((nblk, CONV_HALO + rows, LANES), F32),
                        pltpu.VMEM((nblk, rows, LANES), F32)],
        compiler_params=pltpu.CompilerParams(dimension_semantics=("arbitrary", "arbitrary"),
                                             vmem_limit_bytes=VMEM_LIMIT),
        name="conformer_pre",
    )(u, za, cw, ln_w.reshape(1, c), ln_b.reshape(1, c))


SSD_ROWS = 2 * CHUNK


def _short_conv_silu(buf_ref, w_ref, b_ref, g, row0):
    w = w_ref[g]
    bias = b_ref[g]
    cols = []
    for c in range(buf_ref.shape[1]):
        lanes = slice(c * LANES, (c + 1) * LANES)
        acc = jnp.broadcast_to(bias[:, lanes], (CHUNK, LANES))
        for k in range(SSM_CONV):
            start = row0 + TAIL_ROWS - (SSM_CONV - 1) + k
            acc = acc + w[k:k + 1, lanes] * buf_ref[g, c, pl.ds(start, CHUNK), :]
        cols.append(acc)
    return _silu(jnp.concatenate(cols, axis=1))


def _ssd_kernel(x_ref, b_ref, c_ref, zs_ref, col_ref, et_ref, rowT_ref, expand_ref,
                wx_ref, bx_ref, wb_ref, bb_ref, wc_ref, bc_ref, dsk_ref, nw_ref, wout_ref, gate_ref,
                o_ref,
                state_ref, xbuf, bbuf, cbuf, ynew_ref, yprev_ref, *, steps_per_seq):
    L = CHUNK
    hpg = HEADS_PER_GROUP
    t = pl.program_id(0)

    @pl.when(t % steps_per_seq == 0)
    def _():
        state_ref[...] = jnp.zeros(state_ref.shape, F32)
        for buf in (xbuf, bbuf, cbuf):
            buf[:, :, 0:TAIL_ROWS, :] = jnp.zeros(buf.shape[:2] + (TAIL_ROWS, LANES), F32)

    @pl.when(t == 0)
    def _():
        ynew_ref[...] = jnp.zeros(ynew_ref.shape, ynew_ref.dtype)

    yprev_ref[...] = ynew_ref[...]
    o_ref[...] = (gate_ref[...].astype(F32) * _dot(yprev_ref[...], wout_ref[...])).astype(o_ref.dtype)

    for g in range(N_GROUPS):
        for src, buf in ((x_ref, xbuf), (b_ref, bbuf), (c_ref, cbuf)):
            for c in range(buf.shape[1]):
                buf[g, c, TAIL_ROWS:TAIL_ROWS + SSD_ROWS, :] = src[g, :, c * LANES:(c + 1) * LANES].astype(F32)

    row = lax.broadcasted_iota(jnp.int32, (L, L), 0)
    col = lax.broadcasted_iota(jnp.int32, (L, L), 1)
    causal = row >= col
    low_half = col < HEADDIM

    for ci in range(SSD_ROWS // L):
        row0 = ci * L
        rows = slice(row0, row0 + L)
        colv = col_ref[rows, :]
        etv = et_ref[rows, :]
        for g in range(N_GROUPS):
            xs = _short_conv_silu(xbuf, wx_ref, bx_ref, g, row0)
            bm = _short_conv_silu(bbuf, wb_ref, bb_ref, g, row0)
            cm = _short_conv_silu(cbuf, wc_ref, bc_ref, g, row0)
            bt = bm.T.astype(BF16)
            cm16 = cm.astype(BF16)
            cb16 = _dot(cm16, bt).astype(BF16)
            st = state_ref[g]
            y_state = _dot(cm16, st.astype(BF16))
            et = _dot(etv, expand_ref[g])
            ecol = et[:, :GROUP_W]
            tcol = et[:, GROUP_W:]
            ys = []
            for q in range(hpg // 2):
                xp = xs[:, q * LANES:(q + 1) * LANES]
                ms = []
                for e in range(2):
                    h = g * hpg + 2 * q + e
                    seg = jnp.broadcast_to(colv[:, h:h + 1], (L, L)) - rowT_ref[h:h + 1, rows]
                    decay = jnp.exp2(jnp.where(causal, seg, -jnp.inf))
                    ms.append(cb16 * decay.astype(BF16))
                lhs = jnp.concatenate(ms, axis=1)
                rhs = jnp.concatenate([jnp.where(low_half, xp, 0.0), jnp.where(low_half, 0.0, xp)],
                                      axis=0).astype(BF16)
                ys.append(_dot(lhs, rhs))
            y = jnp.concatenate(ys, axis=1) + y_state * ecol
            state_ref[g] = st * ecol[L - 1:L, :] + _dot(bt, (xs * tcol).astype(BF16))
            y = y + dsk_ref[g] * xs
            y = y * zs_ref[g, rows, :].astype(F32)
            y = y * lax.rsqrt(jnp.mean(y * y, axis=-1, keepdims=True) + NORM_EPS)
            ynew_ref[rows, g * GROUP_W:(g + 1) * GROUP_W] = (y * nw_ref[g]).astype(ynew_ref.dtype)

    for buf in (xbuf, bbuf, cbuf):
        buf[:, :, 0:TAIL_ROWS, :] = buf[:, :, SSD_ROWS:SSD_ROWS + TAIL_ROWS, :]


def _head_expand_matrix():
    src = jnp.arange(4 * N_HEADS)
    head, is_t = src % N_HEADS, (src // N_HEADS) % 2
    dst = jnp.arange(2 * GROUP_W)
    dst_t, dst_head_in_group = dst // GROUP_W, (dst % GROUP_W) // HEADDIM
    g = jnp.arange(N_GROUPS)[:, None, None]
    hit = (head[None, :, None] == g * HEADS_PER_GROUP + dst_head_in_group[None, None, :]) \
        & (is_t[None, :, None] == dst_t[None, None, :])
    return hit.astype(BF16)


def _ssd_branch(xs, bm, cm, zs, colv, et, rowT, conv_w, conv_b, d_skip, norm_w, w_out, gates, gate_col, seq):
    g, m, gw = xs.shape
    n = bm.shape[2]
    nh = colv.shape[1]
    d_in, d_out = w_out.shape
    nblocks = m // SSD_ROWS
    last = nblocks - 1

    def cur(t):
        return jnp.minimum(t, last)

    def prev(t):
        return jnp.maximum(t - 1, 0)

    def grp(width):
        return pl.BlockSpec((g, SSD_ROWS, width), lambda t: (0, cur(t), 0))

    def const(shape):
        return pl.BlockSpec(shape, lambda t: (0,) * len(shape), pipeline_mode=pl.Buffered(1))

    def group_major(v, width):
        return v.reshape(v.shape[0], g, width).transpose(1, 0, 2)

    wx = group_major(conv_w[:, :D_INNER], gw)
    wb = group_major(conv_w[:, D_INNER:D_INNER + g * n], n)
    wc = group_major(conv_w[:, D_INNER + g * n:], n)
    cbias = conv_b.reshape(1, -1)
    bx = group_major(cbias[:, :D_INNER], gw)
    bb = group_major(cbias[:, D_INNER:D_INNER + g * n], n)
    bc = group_major(cbias[:, D_INNER + g * n:], n)
    dsk = jnp.repeat(d_skip, HEADDIM).reshape(g, 1, gw)
    nw = norm_w.reshape(g, 1, gw)
    return pl.pallas_call(
        functools.partial(_ssd_kernel, steps_per_seq=seq // SSD_ROWS),
        out_shape=jax.ShapeDtypeStruct((m, d_out), BF16),
        grid=(nblocks + 1,),
        in_specs=[grp(gw), grp(n), grp(n), grp(gw),
                  pl.BlockSpec((SSD_ROWS, nh), lambda t: (cur(t), 0)),
                  pl.BlockSpec((SSD_ROWS, 4 * nh), lambda t: (cur(t), 0)),
                  pl.BlockSpec((nh, SSD_ROWS), lambda t: (0, cur(t))),
                  const((g, 4 * nh, 2 * gw)),
                  const((g, SSM_CONV, gw)), const((g, 1, gw)),
                  const((g, SSM_CONV, n)), const((g, 1, n)),
                  const((g, SSM_CONV, n)), const((g, 1, n)),
                  const((g, 1, gw)), const((g, 1, gw)),
                  const((d_in, d_out)),
                  pl.BlockSpec((SSD_ROWS, d_out), lambda t: (prev(t), gate_col))],
        out_specs=pl.BlockSpec((SSD_ROWS, d_out), lambda t: (prev(t), 0)),
        scratch_shapes=[pltpu.VMEM((g, n, gw), F32),
                        pltpu.VMEM((g, gw // LANES, TAIL_ROWS + SSD_ROWS, LANES), F32),
                        pltpu.VMEM((g, n // LANES, TAIL_ROWS + SSD_ROWS, LANES), F32),
                        pltpu.VMEM((g, n // LANES, TAIL_ROWS + SSD_ROWS, LANES), F32),
                        pltpu.VMEM((SSD_ROWS, d_in), BF16),
                        pltpu.VMEM((SSD_ROWS, d_in), BF16)],
        compiler_params=pltpu.CompilerParams(dimension_semantics=("arbitrary",),
                                             vmem_limit_bytes=VMEM_LIMIT),
        name="ssd_branch",
    )(xs, bm, cm, zs, colv, et, rowT, _head_expand_matrix(), wx, bx, wb, bb, wc, bc, dsk, nw, w_out,
      gates)


def _final_kernel(a_ref, wa_ref, ga_ref, yb_ref, wo_ref, x_ref, nw_ref, o_ref):
    merged = ga_ref[...].astype(F32) * _dot(a_ref[...], wa_ref[...]) + yb_ref[...].astype(F32)
    x = x_ref[...] + _dot(merged.astype(BF16), wo_ref[...])
    ms = jnp.mean(x * x, axis=-1, keepdims=True)
    o_ref[...] = x * lax.rsqrt(ms + NORM_EPS) * nw_ref[...]


def _final(act_a, w_a, gates, y_b, w_o, x2, norm_w, tm=256):
    m, d = x2.shape
    rows = pl.BlockSpec((tm, d), lambda i: (i, 0))
    weight = pl.BlockSpec((d, d), lambda i: (0, 0), pipeline_mode=pl.Buffered(1))
    return pl.pallas_call(
        _final_kernel,
        out_shape=jax.ShapeDtypeStruct((m, d), F32),
        grid=(m // tm,),
        in_specs=[rows, weight, rows, rows, weight, rows,
                  pl.BlockSpec((1, d), lambda i: (0, 0))],
        out_specs=rows,
        compiler_params=pltpu.CompilerParams(dimension_semantics=("arbitrary",),
                                             vmem_limit_bytes=VMEM_LIMIT),
        name="final_norm",
    )(act_a, w_a, gates, y_b, w_o, x2, norm_w.reshape(1, d))


def _layer(x2, batch, seq, norm_w, w_in, conv_a_w, conv_a_b, ln_a_w, ln_a_b, w_a_out,
           conv_b_w, conv_b_b, dt_bias, a_log, d_skip, ssm_norm_w, w_b_out, w_o):
    w_dt = w_in[:, OFF_S_DT:OFF_G].astype(BF16)
    w_gate = w_in[:, OFF_G:]

    h = _rmsnorm_cast(x2, norm_w)
    u = _proj(h, w_in, OFF_A_VAL, CONV_CH, "glu", gate_col0=OFF_A_GATE, tn=512)
    za = _proj(h, w_in, OFF_A_Z, CONV_CH, "silu")
    zs = _proj(h, w_in, OFF_S_Z, D_INNER, "silu", split_w=GROUP_W)
    xs = _proj(h, w_in, OFF_S_X, D_INNER, "id", split_w=GROUP_W)
    bm = _proj(h, w_in, OFF_S_B, N_GROUPS * D_STATE, "id", split_w=D_STATE)
    cm = _proj(h, w_in, OFF_S_C, N_GROUPS * D_STATE, "id", split_w=D_STATE)
    gates = _proj(h, w_gate, 0, 2 * D_MODEL, "sigmoid")
    colv, et, rowT = _ssd_scalars(h, w_dt, dt_bias, a_log)

    act_a = _conformer_pre(u, za, conv_a_w, conv_a_b, ln_a_w, ln_a_b, batch, seq)
    y_b = _ssd_branch(xs, bm, cm, zs, colv, et, rowT, conv_b_w, conv_b_b, d_skip, ssm_norm_w,
                      w_b_out.astype(BF16), gates, 1, seq)
    return act_a, gates, y_b


def kernel(x, norm_w, w_in, conv_a_w, conv_a_b, ln_a_w, ln_a_b, w_a_out, conv_b_w, conv_b_b, dt_bias, a_log,
           d_skip, ssm_norm_w, w_b_out, w_o, final_norm_w):
    batch, seq, d = x.shape
    depth = norm_w.shape[0]
    assert depth == 1, "the residual add is fused with the final norm for a single layer"
    x2 = x.reshape(batch * seq, d)
    act_a, gates, y_b = _layer(x2, batch, seq, norm_w[0], w_in[0], conv_a_w[0], conv_a_b[0], ln_a_w[0],
                               ln_a_b[0], w_a_out[0], conv_b_w[0], conv_b_b[0], dt_bias[0], a_log[0],
                               d_skip[0], ssm_norm_w[0], w_b_out[0], w_o[0])
    out = _final(act_a, w_a_out[0].astype(BF16), gates, y_b, w_o[0].astype(BF16), x2, final_norm_w)
    return out.reshape(batch, seq, d)
```

```python
import functools
import math

import jax
import jax.numpy as jnp
from jax import lax
from jax.experimental import pallas as pl
from jax.experimental.pallas import tpu as pltpu

F32 = jnp.float32
BF16 = jnp.bfloat16

D_MODEL = 2048
CONV_CH = D_MODEL
CONV_WIDTH = 31
D_INNER = 2 * D_MODEL
HEADDIM = 64
N_HEADS = D_INNER // HEADDIM
D_STATE = 128
N_GROUPS = 8
HEADS_PER_GROUP = N_HEADS // N_GROUPS
GROUP_W = HEADS_PER_GROUP * HEADDIM
SSM_CONV = 4
CHUNK = 128
XBC_DIM = D_INNER + 2 * N_GROUPS * D_STATE
NORM_EPS = 1e-6
LN_EPS = 1e-5
LOG2E = math.log2(math.e)

OFF_A_VAL = 0
OFF_A_GATE = OFF_A_VAL + CONV_CH
OFF_A_Z = OFF_A_GATE + CONV_CH
OFF_S_Z = OFF_A_Z + CONV_CH
OFF_S_X = OFF_S_Z + D_INNER
OFF_S_B = OFF_S_X + D_INNER
OFF_S_C = OFF_S_B + N_GROUPS * D_STATE
OFF_S_DT = OFF_S_C + N_GROUPS * D_STATE
OFF_G = OFF_S_DT + N_HEADS

LANES = 128
CONV_HALO = 32
TAIL_ROWS = 8
VMEM_LIMIT = 56 * 1024 * 1024


def _sigmoid(v):
    return 1.0 / (1.0 + jnp.exp(-v))


def _silu(v):
    return v * _sigmoid(v)


def _softplus(v):
    return jnp.maximum(v, 0.0) + jnp.log1p(jnp.exp(-jnp.abs(v)))


def _split2(v):
    v0 = v.astype(BF16)
    return v0, (v - v0.astype(F32)).astype(BF16)


def _split3(v):
    v0 = v.astype(BF16)
    r = v - v0.astype(F32)
    v1 = r.astype(BF16)
    r = r - v1.astype(F32)
    return v0, v1, r.astype(BF16)


def _dot(a, b):
    return jnp.dot(a, b, preferred_element_type=F32)


def _rmsnorm_kernel(x_ref, w_ref, o_ref):
    x = x_ref[...]
    ms = jnp.mean(x * x, axis=-1, keepdims=True)
    o_ref[...] = (x * lax.rsqrt(ms + NORM_EPS) * w_ref[...]).astype(o_ref.dtype)


def _rmsnorm_cast(x2, w, tm=512):
    m, d = x2.shape
    return pl.pallas_call(
        _rmsnorm_kernel,
        out_shape=jax.ShapeDtypeStruct((m, d), BF16),
        grid=(m // tm,),
        in_specs=[pl.BlockSpec((tm, d), lambda i: (i, 0)),
                  pl.BlockSpec((1, d), lambda i: (0, 0))],
        out_specs=pl.BlockSpec((tm, d), lambda i: (i, 0)),
        compiler_params=pltpu.CompilerParams(dimension_semantics=("arbitrary",),
                                             vmem_limit_bytes=VMEM_LIMIT),
        name="rmsnorm_cast",
    )(x2, w.reshape(1, d))


def _dot_nt(a, b):
    return lax.dot_general(a, b, (((1,), (1,)), ((), ())), preferred_element_type=F32)


def _proj_kernel(*refs, mode, nsplit, shift):
    if mode == "glu" or shift:
        h_ref, w_ref, w2_ref, o_ref, w16_ref = refs[:5]
    else:
        h_ref, w_ref, o_ref, w16_ref = refs

    @pl.when(pl.program_id(1) == 0)
    def _():
        if shift:
            w16_ref[...] = jnp.concatenate([w_ref[shift:, :], w2_ref[:shift, :]], axis=0).astype(BF16)
        else:
            w16_ref[...] = w_ref[...].astype(BF16)
        if mode == "glu":
            refs[5][...] = w2_ref[...].astype(BF16)

    h = h_ref[...]
    acc = _dot_nt(h, w16_ref[...])
    if mode == "glu":
        res = acc * _sigmoid(_dot_nt(h, refs[5][...]))
    elif mode == "silu":
        res = _silu(acc)
    elif mode == "sigmoid":
        res = _sigmoid(acc)
    else:
        res = acc
    width = res.shape[1] // nsplit
    for s in range(nsplit):
        o_ref[s] = res[:, s * width:(s + 1) * width].astype(o_ref.dtype)


def _proj(h, wt, col0, ncols, mode, *, split_w=None, gate_col0=None, tm=1024, tn=1024):
    m, k = h.shape
    cb0, shift = divmod(col0, tn)
    in_specs = [pl.BlockSpec((tm, k), lambda j, i: (i, 0)),
                pl.BlockSpec((tn, k), lambda j, i: (cb0 + j, 0))]
    args = [h, wt]
    scratch = [pltpu.VMEM((tn, k), BF16)]
    if shift:
        assert mode != "glu" and shift % 8 == 0
        in_specs.append(pl.BlockSpec((tn, k), lambda j, i: (cb0 + j + 1, 0)))
        args.append(wt)
    if mode == "glu":
        gb0 = gate_col0 // tn
        in_specs.append(pl.BlockSpec((tn, k), lambda j, i: (gb0 + j, 0)))
        args.append(wt)
        scratch.append(pltpu.VMEM((tn, k), BF16))
    if split_w is None:
        nsplit = 1
        out_shape = jax.ShapeDtypeStruct((1, m, ncols), BF16)
        out_spec = pl.BlockSpec((1, tm, tn), lambda j, i: (0, i, j))
    else:
        nsplit = tn // split_w
        out_shape = jax.ShapeDtypeStruct((ncols // split_w, m, split_w), BF16)
        out_spec = pl.BlockSpec((nsplit, tm, split_w), lambda j, i: (j, i, 0))
    out = pl.pallas_call(
        functools.partial(_proj_kernel, mode=mode, nsplit=nsplit, shift=shift),
        out_shape=out_shape,
        grid=(ncols // tn, m // tm),
        in_specs=in_specs,
        out_specs=out_spec,
        scratch_shapes=scratch,
        compiler_params=pltpu.CompilerParams(dimension_semantics=("arbitrary", "arbitrary"),
                                             vmem_limit_bytes=VMEM_LIMIT),
        name="proj_" + mode,
    )(*args)
    return out[0] if split_w is None else out


def _scalars_kernel(h_ref, w_ref, bias_ref, alog_ref, col_ref, et_ref, rowT_ref):
    tm = h_ref.shape[0]
    nh = bias_ref.shape[1]
    dt = _softplus(_dot_nt(h_ref[...], w_ref[...].astype(BF16)) + bias_ref[...])
    da = dt * (-jnp.exp(alog_ref[...]))
    log2dt = jnp.log(dt) * LOG2E
    row = lax.broadcasted_iota(jnp.int32, (CHUNK, CHUNK), 0)
    col = lax.broadcasted_iota(jnp.int32, (CHUNK, CHUNK), 1)
    tril = (row >= col).astype(BF16)
    for c in range(tm // CHUNK):
        sl = slice(c * CHUNK, (c + 1) * CHUNK)
        d0, d1, d2 = _split3(da[sl])
        cum = _dot(tril, d0) + _dot(tril, d1) + _dot(tril, d2)
        cum2 = cum * LOG2E
        col_ref[sl, :] = cum2
        e_hi, e_lo = _split2(jnp.exp(cum))
        t_hi, t_lo = _split2(jnp.exp(cum[CHUNK - 1:CHUNK, :] - cum) * dt[sl])
        et_ref[sl, :] = jnp.concatenate([e_hi, t_hi, e_lo, t_lo], axis=1)
        both = jnp.concatenate([cum2 - log2dt[sl], cum2], axis=1)
        rowT_ref[:, sl] = both.T[:nh]


def _ssd_scalars(h, wt, col0, dt_bias, a_log, tm=1024):
    m, k = h.shape
    nh = dt_bias.shape[0]
    assert col0 % nh == 0
    return pl.pallas_call(
        _scalars_kernel,
        out_shape=(jax.ShapeDtypeStruct((m, nh), F32),
                   jax.ShapeDtypeStruct((m, 4 * nh), BF16),
                   jax.ShapeDtypeStruct((nh, m), F32)),
        grid=(m // tm,),
        in_specs=[pl.BlockSpec((tm, k), lambda i: (i, 0)),
                  pl.BlockSpec((nh, k), lambda i: (col0 // nh, 0)),
                  pl.BlockSpec((1, nh), lambda i: (0, 0)),
                  pl.BlockSpec((1, nh), lambda i: (0, 0))],
        out_specs=(pl.BlockSpec((tm, nh), lambda i: (i, 0)),
                   pl.BlockSpec((tm, 4 * nh), lambda i: (i, 0)),
                   pl.BlockSpec((nh, tm), lambda i: (0, i))),
        compiler_params=pltpu.CompilerParams(dimension_semantics=("arbitrary",),
                                             vmem_limit_bytes=VMEM_LIMIT),
        name="ssd_scalars",
    )(h, wt, dt_bias.reshape(1, nh), a_log.reshape(1, nh))


def _conf_kernel(u_ref, za_ref, cw_ref, lnw_ref, lnb_ref, o_ref, ubuf, cbuf, *, rows, conv_rows, ln_rows):
    nblk = ubuf.shape[0]

    @pl.when(pl.program_id(1) == 0)
    def _():
        ubuf[:, 0:CONV_HALO, :] = jnp.zeros((nblk, CONV_HALO, LANES), F32)

    for j in range(nblk):
        ubuf[j, CONV_HALO:CONV_HALO + rows, :] = u_ref[:, j * LANES:(j + 1) * LANES].astype(F32)

    first = CONV_HALO - (CONV_WIDTH - 1)

    def conv_block(j, carry):
        w = cw_ref[j]
        for r0 in range(0, rows, conv_rows):
            acc = jnp.broadcast_to(w[CONV_WIDTH:CONV_WIDTH + 1, :], (conv_rows, LANES))
            for k in range(CONV_WIDTH):
                acc = acc + w[k:k + 1, :] * ubuf[j, pl.ds(r0 + first + k, conv_rows), :]
            cbuf[j, r0:r0 + conv_rows, :] = acc
        ubuf[j, 0:CONV_HALO, :] = ubuf[j, rows:rows + CONV_HALO, :]
        return carry

    lax.fori_loop(0, nblk, conv_block, 0)

    inv_c = 1.0 / (nblk * LANES)

    def ln_block(i, carry):
        r0 = pl.multiple_of(i * ln_rows, ln_rows)
        rows_i = pl.ds(r0, ln_rows)
        total = cbuf[0, rows_i, :]
        for j in range(1, nblk):
            total = total + cbuf[j, rows_i, :]
        mu = jnp.sum(total, axis=-1, keepdims=True) * inv_c
        sq = jnp.zeros((ln_rows, LANES), F32)
        for j in range(nblk):
            d = cbuf[j, rows_i, :] - mu
            sq = sq + d * d
        rstd = lax.rsqrt(jnp.sum(sq, axis=-1, keepdims=True) * inv_c + LN_EPS)
        for j in range(nblk):
            lanes = slice(j * LANES, (j + 1) * LANES)
            v = (cbuf[j, rows_i, :] - mu) * rstd * lnw_ref[:, lanes] + lnb_ref[:, lanes]
            v = _silu(v) * za_ref[rows_i, lanes].astype(F32)
            o_ref[rows_i, lanes] = v.astype(o_ref.dtype)
        return carry

    lax.fori_loop(0, rows // ln_rows, ln_block, 0, unroll=2)


def _conformer_pre(u, za, conv_w, conv_b, ln_w, ln_b, batch, seq, rows=256, conv_rows=64, ln_rows=64):
    m, c = u.shape
    nblk = c // LANES
    assert conv_w.shape[0] == CONV_WIDTH and CONV_WIDTH + 1 <= CONV_HALO
    cw = jnp.concatenate([conv_w, conv_b.reshape(1, c)], axis=0)
    cw = jnp.pad(cw, ((0, CONV_HALO - CONV_WIDTH - 1), (0, 0))).reshape(CONV_HALO, nblk, LANES).transpose(1, 0, 2)
    nt = seq // rows
    tile = pl.BlockSpec((rows, c), lambda b, t: (b * nt + t, 0))
    vec = pl.BlockSpec((1, c), lambda b, t: (0, 0))
    return pl.pallas_call(
        functools.partial(_conf_kernel, rows=rows, conv_rows=conv_rows, ln_rows=ln_rows),
        out_shape=jax.ShapeDtypeStruct((m, c), BF16),
        grid=(batch, nt),
        in_specs=[tile, tile,
                  pl.BlockSpec((nblk, CONV_HALO, LANES), lambda b, t: (0, 0, 0)),
                  vec, vec],
        out_specs=tile,
        scratch_shapes=[pltpu.VMEM((nblk, CONV_HALO + rows, LANES), F32),
                        pltpu.VMEM((nblk, rows, LANES), F32)],
        compiler_params=pltpu.CompilerParams(dimension_semantics=("arbitrary", "arbitrary"),
                                             vmem_limit_bytes=VMEM_LIMIT),
        name="conformer_pre",
    )(u, za, cw, ln_w.reshape(1, c), ln_b.reshape(1, c))


SSD_ROWS = 2 * CHUNK


def _short_conv_silu(buf_ref, w_ref, b_ref, g, row0):
    w = w_ref[g]
    bias = b_ref[g]
    cols = []
    for c in range(buf_ref.shape[1]):
        lanes = slice(c * LANES, (c + 1) * LANES)
        acc = jnp.broadcast_to(bias[:, lanes], (CHUNK, LANES))
        for k in range(SSM_CONV):
            start = row0 + TAIL_ROWS - (SSM_CONV - 1) + k
            acc = acc + w[k:k + 1, lanes] * buf_ref[g, c, pl.ds(start, CHUNK), :]
        cols.append(acc)
    return _silu(jnp.concatenate(cols, axis=1))


def _ssd_kernel(x_ref, b_ref, c_ref, zs_ref, col_ref, et_ref, rowT_ref, expand_ref,
                wx_ref, bx_ref, wb_ref, bb_ref, wc_ref, bc_ref, dsk_ref, nw_ref, wout_ref, gate_ref,
                o_ref,
                state_ref, xbuf, bbuf, cbuf, ynew_ref, yprev_ref, *, steps_per_seq):
    L = CHUNK
    hpg = HEADS_PER_GROUP
    t = pl.program_id(0)

    @pl.when(t % steps_per_seq == 0)
    def _():
        state_ref[...] = jnp.zeros(state_ref.shape, F32)
        for buf in (xbuf, bbuf, cbuf):
            buf[:, :, 0:TAIL_ROWS, :] = jnp.zeros(buf.shape[:2] + (TAIL_ROWS, LANES), F32)

    @pl.when(t == 0)
    def _():
        ynew_ref[...] = jnp.zeros(ynew_ref.shape, ynew_ref.dtype)

    yprev_ref[...] = ynew_ref[...]
    slice_w = o_ref.shape[1] // N_GROUPS

    def out_proj_slice(i):
        cols = slice(i * slice_w, (i + 1) * slice_w)
        proj = _dot(yprev_ref[...], wout_ref[:, cols])
        o_ref[:, cols] = (gate_ref[:, cols].astype(F32) * proj).astype(o_ref.dtype)

    for g in range(N_GROUPS):
        for src, buf in ((x_ref, xbuf), (b_ref, bbuf), (c_ref, cbuf)):
            for c in range(buf.shape[1]):
                buf[g, c, TAIL_ROWS:TAIL_ROWS + SSD_ROWS, :] = src[g, :, c * LANES:(c + 1) * LANES].astype(F32)

    row = lax.broadcasted_iota(jnp.int32, (L, L), 0)
    col = lax.broadcasted_iota(jnp.int32, (L, L), 1)
    causal = row >= col
    low_half = col < HEADDIM

    for g in range(N_GROUPS):
        out_proj_slice(g)
        et_all = _dot(et_ref[...], expand_ref[g])
        for ci in range(SSD_ROWS // L):
            row0 = ci * L
            rows = slice(row0, row0 + L)
            colv = col_ref[rows, :]
            xs = _short_conv_silu(xbuf, wx_ref, bx_ref, g, row0)
            bm = _short_conv_silu(bbuf, wb_ref, bb_ref, g, row0)
            cm = _short_conv_silu(cbuf, wc_ref, bc_ref, g, row0)
            bt = bm.T.astype(BF16)
            cm16 = cm.astype(BF16)
            cb16 = _dot(cm16, bt).astype(BF16)
            st = state_ref[g]
            y_state = _dot(cm16, st.astype(BF16))
            ecol = et_all[rows, :GROUP_W]
            tcol = et_all[rows, GROUP_W:]
            ys = []
            for q in range(hpg // 2):
                xp = xs[:, q * LANES:(q + 1) * LANES]
                ms = []
                for e in range(2):
                    h = g * hpg + 2 * q + e
                    seg = jnp.broadcast_to(colv[:, h:h + 1], (L, L)) - rowT_ref[h:h + 1, rows]
                    decay = jnp.exp2(jnp.where(causal, seg, -jnp.inf))
                    ms.append(cb16 * decay.astype(BF16))
                lhs = jnp.concatenate(ms, axis=1)
                rhs = jnp.concatenate([jnp.where(low_half, xp, 0.0), jnp.where(low_half, 0.0, xp)],
                                      axis=0).astype(BF16)
                ys.append(_dot(lhs, rhs))
            y = jnp.concatenate(ys, axis=1) + y_state * ecol
            state_ref[g] = st * ecol[L - 1:L, :] + _dot(bt, (xs * tcol).astype(BF16))
            y = y + dsk_ref[g] * xs
            y = y * zs_ref[g, rows, :].astype(F32)
            y = y * lax.rsqrt(jnp.mean(y * y, axis=-1, keepdims=True) + NORM_EPS)
            ynew_ref[rows, g * GROUP_W:(g + 1) * GROUP_W] = (y * nw_ref[g]).astype(ynew_ref.dtype)

    for buf in (xbuf, bbuf, cbuf):
        buf[:, :, 0:TAIL_ROWS, :] = buf[:, :, SSD_ROWS:SSD_ROWS + TAIL_ROWS, :]


def _head_expand_matrix():
    src = jnp.arange(4 * N_HEADS)
    head, is_t = src % N_HEADS, (src // N_HEADS) % 2
    dst = jnp.arange(2 * GROUP_W)
    dst_t, dst_head_in_group = dst // GROUP_W, (dst % GROUP_W) // HEADDIM
    g = jnp.arange(N_GROUPS)[:, None, None]
    hit = (head[None, :, None] == g * HEADS_PER_GROUP + dst_head_in_group[None, None, :]) \
        & (is_t[None, :, None] == dst_t[None, None, :])
    return hit.astype(BF16)


def _ssd_branch(xs, bm, cm, zs, colv, et, rowT, conv_w, conv_b, d_skip, norm_w, w_out, gates, gate_col, seq):
    g, m, gw = xs.shape
    n = bm.shape[2]
    nh = colv.shape[1]
    d_in, d_out = w_out.shape
    nblocks = m // SSD_ROWS
    last = nblocks - 1

    def cur(t):
        return jnp.minimum(t, last)

    def prev(t):
        return jnp.maximum(t - 1, 0)

    def grp(width):
        return pl.BlockSpec((g, SSD_ROWS, width), lambda t: (0, cur(t), 0))

    def const(shape):
        return pl.BlockSpec(shape, lambda t: (0,) * len(shape), pipeline_mode=pl.Buffered(1))

    def group_major(v, width):
        return v.reshape(v.shape[0], g, width).transpose(1, 0, 2)

    wx = group_major(conv_w[:, :D_INNER], gw)
    wb = group_major(conv_w[:, D_INNER:D_INNER + g * n], n)
    wc = group_major(conv_w[:, D_INNER + g * n:], n)
    cbias = conv_b.reshape(1, -1)
    bx = group_major(cbias[:, :D_INNER], gw)
    bb = group_major(cbias[:, D_INNER:D_INNER + g * n], n)
    bc = group_major(cbias[:, D_INNER + g * n:], n)
    dsk = jnp.repeat(d_skip, HEADDIM).reshape(g, 1, gw)
    nw = norm_w.reshape(g, 1, gw)
    return pl.pallas_call(
        functools.partial(_ssd_kernel, steps_per_seq=seq // SSD_ROWS),
        out_shape=jax.ShapeDtypeStruct((m, d_out), BF16),
        grid=(nblocks + 1,),
        in_specs=[grp(gw), grp(n), grp(n), grp(gw),
                  pl.BlockSpec((SSD_ROWS, nh), lambda t: (cur(t), 0)),
                  pl.BlockSpec((SSD_ROWS, 4 * nh), lambda t: (cur(t), 0)),
                  pl.BlockSpec((nh, SSD_ROWS), lambda t: (0, cur(t))),
                  const((g, 4 * nh, 2 * gw)),
                  const((g, SSM_CONV, gw)), const((g, 1, gw)),
                  const((g, SSM_CONV, n)), const((g, 1, n)),
                  const((g, SSM_CONV, n)), const((g, 1, n)),
                  const((g, 1, gw)), const((g, 1, gw)),
                  const((d_in, d_out)),
                  pl.BlockSpec((SSD_ROWS, d_out), lambda t: (prev(t), gate_col))],
        out_specs=pl.BlockSpec((SSD_ROWS, d_out), lambda t: (prev(t), 0)),
        scratch_shapes=[pltpu.VMEM((g, n, gw), F32),
                        pltpu.VMEM((g, gw // LANES, TAIL_ROWS + SSD_ROWS, LANES), F32),
                        pltpu.VMEM((g, n // LANES, TAIL_ROWS + SSD_ROWS, LANES), F32),
                        pltpu.VMEM((g, n // LANES, TAIL_ROWS + SSD_ROWS, LANES), F32),
                        pltpu.VMEM((SSD_ROWS, d_in), BF16),
                        pltpu.VMEM((SSD_ROWS, d_in), BF16)],
        compiler_params=pltpu.CompilerParams(dimension_semantics=("arbitrary",),
                                             vmem_limit_bytes=VMEM_LIMIT),
        name="ssd_branch",
    )(xs, bm, cm, zs, colv, et, rowT, _head_expand_matrix(), wx, bx, wb, bb, wc, bc, dsk, nw, w_out,
      gates)


def _final_kernel(a_ref, wa_ref, ga_ref, yb_ref, wo_ref, x_ref, nw_ref, o_ref):
    merged = ga_ref[...].astype(F32) * _dot(a_ref[...], wa_ref[...]) + yb_ref[...].astype(F32)
    x = x_ref[...] + _dot(merged.astype(BF16), wo_ref[...])
    ms = jnp.mean(x * x, axis=-1, keepdims=True)
    o_ref[...] = x * lax.rsqrt(ms + NORM_EPS) * nw_ref[...]


def _final(act_a, w_a, gates, y_b, w_o, x2, norm_w, tm=256):
    m, d = x2.shape
    rows = pl.BlockSpec((tm, d), lambda i: (i, 0))
    weight = pl.BlockSpec((d, d), lambda i: (0, 0), pipeline_mode=pl.Buffered(1))
    return pl.pallas_call(
        _final_kernel,
        out_shape=jax.ShapeDtypeStruct((m, d), F32),
        grid=(m // tm,),
        in_specs=[rows, weight, rows, rows, weight, rows,
                  pl.BlockSpec((1, d), lambda i: (0, 0))],
        out_specs=rows,
        compiler_params=pltpu.CompilerParams(dimension_semantics=("arbitrary",),
                                             vmem_limit_bytes=VMEM_LIMIT),
        name="final_norm",
    )(act_a, w_a, gates, y_b, w_o, x2, norm_w.reshape(1, d))


def _layer(x2, batch, seq, norm_w, w_in, conv_a_w, conv_a_b, ln_a_w, ln_a_b, w_a_out,
           conv_b_w, conv_b_b, dt_bias, a_log, d_skip, ssm_norm_w, w_b_out, w_o):
    w_in_t = w_in.T
    h = _rmsnorm_cast(x2, norm_w)
    u = _proj(h, w_in_t, OFF_A_VAL, CONV_CH, "glu", gate_col0=OFF_A_GATE, tn=512)
    za = _proj(h, w_in_t, OFF_A_Z, CONV_CH, "silu")
    zs = _proj(h, w_in_t, OFF_S_Z, D_INNER, "silu", split_w=GROUP_W)
    xs = _proj(h, w_in_t, OFF_S_X, D_INNER, "id", split_w=GROUP_W)
    bm = _proj(h, w_in_t, OFF_S_B, N_GROUPS * D_STATE, "id", split_w=D_STATE)
    cm = _proj(h, w_in_t, OFF_S_C, N_GROUPS * D_STATE, "id", split_w=D_STATE)
    gates = _proj(h, w_in_t, OFF_G, 2 * D_MODEL, "sigmoid", tn=512)
    colv, et, rowT = _ssd_scalars(h, w_in_t, OFF_S_DT, dt_bias, a_log)

    act_a = _conformer_pre(u, za, conv_a_w, conv_a_b, ln_a_w, ln_a_b, batch, seq)
    y_b = _ssd_branch(xs, bm, cm, zs, colv, et, rowT, conv_b_w, conv_b_b, d_skip, ssm_norm_w,
                      w_b_out.astype(BF16), gates, 1, seq)
    return act_a, gates, y_b


def kernel(x, norm_w, w_in, conv_a_w, conv_a_b, ln_a_w, ln_a_b, w_a_out, conv_b_w, conv_b_b, dt_bias, a_log,
           d_skip, ssm_norm_w, w_b_out, w_o, final_norm_w):
    batch, seq, d = x.shape
    depth = norm_w.shape[0]
    assert depth == 1, "the residual add is fused with the final norm for a single layer"
    x2 = x.reshape(batch * seq, d)
    act_a, gates, y_b = _layer(x2, batch, seq, norm_w[0], w_in[0], conv_a_w[0], conv_a_b[0], ln_a_w[0],
                               ln_a_b[0], w_a_out[0], conv_b_w[0], conv_b_b[0], dt_bias[0], a_log[0],
                               d_skip[0], ssm_norm_w[0], w_b_out[0], w_o[0])
    out = _final(act_a, w_a_out[0].astype(BF16), gates, y_b, w_o[0].astype(BF16), x2, final_norm_w)
    return out.reshape(batch, seq, d)
```

```python
import functools
import math

import jax
import jax.numpy as jnp
from jax import lax
from jax.experimental import pallas as pl
from jax.experimental.pallas import tpu as pltpu

F32 = jnp.float32
BF16 = jnp.bfloat16

D_MODEL = 2048
CONV_CH = D_MODEL
CONV_WIDTH = 31
D_INNER = 2 * D_MODEL
HEADDIM = 64
N_HEADS = D_INNER // HEADDIM
D_STATE = 128
N_GROUPS = 8
HEADS_PER_GROUP = N_HEADS // N_GROUPS
GROUP_W = HEADS_PER_GROUP * HEADDIM
SSM_CONV = 4
CHUNK = 128
XBC_DIM = D_INNER + 2 * N_GROUPS * D_STATE
NORM_EPS = 1e-6
LN_EPS = 1e-5
LOG2E = math.log2(math.e)

OFF_A_VAL = 0
OFF_A_GATE = OFF_A_VAL + CONV_CH
OFF_A_Z = OFF_A_GATE + CONV_CH
OFF_S_Z = OFF_A_Z + CONV_CH
OFF_S_X = OFF_S_Z + D_INNER
OFF_S_B = OFF_S_X + D_INNER
OFF_S_C = OFF_S_B + N_GROUPS * D_STATE
OFF_S_DT = OFF_S_C + N_GROUPS * D_STATE
OFF_G = OFF_S_DT + N_HEADS

LANES = 128
CONV_HALO = 32
TAIL_ROWS = 8
VMEM_LIMIT = 56 * 1024 * 1024


def _sigmoid(v):
    return 1.0 / (1.0 + jnp.exp(-v))


def _silu(v):
    return v * _sigmoid(v)


def _softplus(v):
    return jnp.maximum(v, 0.0) + jnp.log1p(jnp.exp(-jnp.abs(v)))


def _split2(v):
    v0 = v.astype(BF16)
    return v0, (v - v0.astype(F32)).astype(BF16)


def _split3(v):
    v0 = v.astype(BF16)
    r = v - v0.astype(F32)
    v1 = r.astype(BF16)
    r = r - v1.astype(F32)
    return v0, v1, r.astype(BF16)


def _dot(a, b):
    return jnp.dot(a, b, preferred_element_type=F32)


def _dot_nt(a, b):
    return lax.dot_general(a, b, (((1,), (1,)), ((), ())), preferred_element_type=F32)


def _rmsnorm_kernel(x_ref, w_ref, o_ref):
    x = x_ref[...]
    ms = jnp.mean(x * x, axis=-1, keepdims=True)
    o_ref[...] = (x * lax.rsqrt(ms + NORM_EPS) * w_ref[...]).astype(o_ref.dtype)


def _rmsnorm_cast(x2, w, tm=512):
    m, d = x2.shape
    return pl.pallas_call(
        _rmsnorm_kernel,
        out_shape=jax.ShapeDtypeStruct((m, d), BF16),
        grid=(m // tm,),
        in_specs=[pl.BlockSpec((tm, d), lambda i: (i, 0)),
                  pl.BlockSpec((1, d), lambda i: (0, 0))],
        out_specs=pl.BlockSpec((tm, d), lambda i: (i, 0)),
        compiler_params=pltpu.CompilerParams(dimension_semantics=("arbitrary",),
                                             vmem_limit_bytes=VMEM_LIMIT),
        name="rmsnorm_cast",
    )(x2, w.reshape(1, d))


def _proj_kernel(*refs, mode, nsplit, shift):
    if mode == "glu":
        h_ref, w_ref, w2_ref, o_ref, w16_ref, w216_ref = refs
    elif shift:
        h_ref, w_ref, w2_ref, o_ref, w16_ref = refs
    else:
        h_ref, w_ref, o_ref, w16_ref = refs

    @pl.when(pl.program_id(1) == 0)
    def _():
        if shift:
            w16_ref[...] = jnp.concatenate([w_ref[shift:, :], w2_ref[...]], axis=0).astype(BF16)
        else:
            w16_ref[...] = w_ref[...].astype(BF16)
        if mode == "glu":
            w216_ref[...] = w2_ref[...].astype(BF16)

    h = h_ref[...]
    acc = _dot_nt(h, w16_ref[...])
    if mode == "glu":
        res = acc * _sigmoid(_dot_nt(h, w216_ref[...]))
    elif mode == "silu":
        res = _silu(acc)
    elif mode == "sigmoid":
        res = _sigmoid(acc)
    else:
        res = acc
    width = res.shape[1] // nsplit
    for s in range(nsplit):
        o_ref[s] = res[:, s * width:(s + 1) * width].astype(o_ref.dtype)


def _proj(h, wt, col0, ncols, mode, *, split_w=None, gate_col0=None, tm=1024, tn=1024):
    m, k = h.shape
    cb0, shift = divmod(col0, tn)
    in_specs = [pl.BlockSpec((tm, k), lambda j, i: (i, 0)),
                pl.BlockSpec((tn, k), lambda j, i: (cb0 + j, 0))]
    args = [h, wt]
    scratch = [pltpu.VMEM((tn, k), BF16)]
    if shift:
        assert mode != "glu" and shift % 8 == 0 and tn % shift == 0
        in_specs.append(pl.BlockSpec((shift, k), lambda j, i: ((cb0 + j + 1) * (tn // shift), 0)))
        args.append(wt)
    if mode == "glu":
        gb0 = gate_col0 // tn
        in_specs.append(pl.BlockSpec((tn, k), lambda j, i: (gb0 + j, 0)))
        args.append(wt)
        scratch.append(pltpu.VMEM((tn, k), BF16))
    if split_w is None:
        nsplit = 1
        out_shape = jax.ShapeDtypeStruct((1, m, ncols), BF16)
        out_spec = pl.BlockSpec((1, tm, tn), lambda j, i: (0, i, j))
    else:
        nsplit = tn // split_w
        out_shape = jax.ShapeDtypeStruct((ncols // split_w, m, split_w), BF16)
        out_spec = pl.BlockSpec((nsplit, tm, split_w), lambda j, i: (j, i, 0))
    out = pl.pallas_call(
        functools.partial(_proj_kernel, mode=mode, nsplit=nsplit, shift=shift),
        out_shape=out_shape,
        grid=(ncols // tn, m // tm),
        in_specs=in_specs,
        out_specs=out_spec,
        scratch_shapes=scratch,
        compiler_params=pltpu.CompilerParams(dimension_semantics=("arbitrary", "arbitrary"),
                                             vmem_limit_bytes=VMEM_LIMIT),
        name="proj_" + mode,
    )(*args)
    return out[0] if split_w is None else out


def _scalars_kernel(h_ref, w_ref, bias_ref, alog_ref, col_ref, et_ref, rowT_ref):
    tm = h_ref.shape[0]
    nh = bias_ref.shape[1]
    dt = _softplus(_dot_nt(h_ref[...], w_ref[...].astype(BF16)) + bias_ref[...])
    da = dt * (-jnp.exp(alog_ref[...]))
    log2dt = jnp.log(dt) * LOG2E
    row = lax.broadcasted_iota(jnp.int32, (CHUNK, CHUNK), 0)
    col = lax.broadcasted_iota(jnp.int32, (CHUNK, CHUNK), 1)
    tril = (row >= col).astype(BF16)
    for c in range(tm // CHUNK):
        sl = slice(c * CHUNK, (c + 1) * CHUNK)
        d0, d1, d2 = _split3(da[sl])
        cum = _dot(tril, d0) + _dot(tril, d1) + _dot(tril, d2)
        cum2 = cum * LOG2E
        col_ref[sl, :] = cum2
        e_hi, e_lo = _split2(jnp.exp(cum))
        t_hi, t_lo = _split2(jnp.exp(cum[CHUNK - 1:CHUNK, :] - cum) * dt[sl])
        et_ref[sl, :] = jnp.concatenate([e_hi, t_hi, e_lo, t_lo], axis=1)
        both = jnp.concatenate([cum2 - log2dt[sl], cum2], axis=1)
        rowT_ref[:, sl] = both.T[:nh]


def _ssd_scalars(h, wt, col0, dt_bias, a_log, tm=1024):
    m, k = h.shape
    nh = dt_bias.shape[0]
    assert col0 % nh == 0
    return pl.pallas_call(
        _scalars_kernel,
        out_shape=(jax.ShapeDtypeStruct((m, nh), F32),
                   jax.ShapeDtypeStruct((m, 4 * nh), BF16),
                   jax.ShapeDtypeStruct((nh, m), F32)),
        grid=(m // tm,),
        in_specs=[pl.BlockSpec((tm, k), lambda i: (i, 0)),
                  pl.BlockSpec((nh, k), lambda i: (col0 // nh, 0)),
                  pl.BlockSpec((1, nh), lambda i: (0, 0)),
                  pl.BlockSpec((1, nh), lambda i: (0, 0))],
        out_specs=(pl.BlockSpec((tm, nh), lambda i: (i, 0)),
                   pl.BlockSpec((tm, 4 * nh), lambda i: (i, 0)),
                   pl.BlockSpec((nh, tm), lambda i: (0, i))),
        compiler_params=pltpu.CompilerParams(dimension_semantics=("arbitrary",),
                                             vmem_limit_bytes=VMEM_LIMIT),
        name="ssd_scalars",
    )(h, wt, dt_bias.reshape(1, nh), a_log.reshape(1, nh))


def _conf_kernel(u_ref, za_ref, cw_ref, lnw_ref, lnb_ref, o_ref, ubuf, cbuf, *, rows, conv_rows, ln_rows):
    nblk = ubuf.shape[0]

    @pl.when(pl.program_id(1) == 0)
    def _():
        ubuf[:, 0:CONV_HALO, :] = jnp.zeros((nblk, CONV_HALO, LANES), F32)

    for j in range(nblk):
        ubuf[j, CONV_HALO:CONV_HALO + rows, :] = u_ref[:, j * LANES:(j + 1) * LANES].astype(F32)

    first = CONV_HALO - (CONV_WIDTH - 1)

    def conv_block(j, carry):
        w = cw_ref[j]
        for r0 in range(0, rows, conv_rows):
            acc = jnp.broadcast_to(w[CONV_WIDTH:CONV_WIDTH + 1, :], (conv_rows, LANES))
            for k in range(CONV_WIDTH):
                acc = acc + w[k:k + 1, :] * ubuf[j, pl.ds(r0 + first + k, conv_rows), :]
            cbuf[j, r0:r0 + conv_rows, :] = acc
        ubuf[j, 0:CONV_HALO, :] = ubuf[j, rows:rows + CONV_HALO, :]
        return carry

    lax.fori_loop(0, nblk, conv_block, 0)

    inv_c = 1.0 / (nblk * LANES)

    def ln_block(i, carry):
        r0 = pl.multiple_of(i * ln_rows, ln_rows)
        rows_i = pl.ds(r0, ln_rows)
        total = cbuf[0, rows_i, :]
        for j in range(1, nblk):
            total = total + cbuf[j, rows_i, :]
        mu = jnp.sum(total, axis=-1, keepdims=True) * inv_c
        sq = jnp.zeros((ln_rows, LANES), F32)
        for j in range(nblk):
            d = cbuf[j, rows_i, :] - mu
            sq = sq + d * d
        rstd = lax.rsqrt(jnp.sum(sq, axis=-1, keepdims=True) * inv_c + LN_EPS)
        for j in range(nblk):
            lanes = slice(j * LANES, (j + 1) * LANES)
            v = (cbuf[j, rows_i, :] - mu) * rstd * lnw_ref[:, lanes] + lnb_ref[:, lanes]
            v = _silu(v) * za_ref[rows_i, lanes].astype(F32)
            o_ref[rows_i, lanes] = v.astype(o_ref.dtype)
        return carry

    lax.fori_loop(0, rows // ln_rows, ln_block, 0, unroll=2)


def _conformer_pre(u, za, conv_w, conv_b, ln_w, ln_b, batch, seq, rows=256, conv_rows=64, ln_rows=64):
    m, c = u.shape
    nblk = c // LANES
    assert conv_w.shape[0] == CONV_WIDTH and CONV_WIDTH + 1 <= CONV_HALO
    cw = jnp.concatenate([conv_w, conv_b.reshape(1, c)], axis=0)
    cw = jnp.pad(cw, ((0, CONV_HALO - CONV_WIDTH - 1), (0, 0))).reshape(CONV_HALO, nblk, LANES).transpose(1, 0, 2)
    nt = seq // rows
    tile = pl.BlockSpec((rows, c), lambda b, t: (b * nt + t, 0))
    vec = pl.BlockSpec((1, c), lambda b, t: (0, 0))
    return pl.pallas_call(
        functools.partial(_conf_kernel, rows=rows, conv_rows=conv_rows, ln_rows=ln_rows),
        out_shape=jax.ShapeDtypeStruct((m, c), BF16),
        grid=(batch, nt),
        in_specs=[tile, tile,
                  pl.BlockSpec((nblk, CONV_HALO, LANES), lambda b, t: (0, 0, 0)),
                  vec, vec],
        out_specs=tile,
        scratch_shapes=[pltpu.VMEM((nblk, CONV_HALO + rows, LANES), F32),
                        pltpu.VMEM((nblk, rows, LANES), F32)],
        compiler_params=pltpu.CompilerParams(dimension_semantics=("arbitrary", "arbitrary"),
                                             vmem_limit_bytes=VMEM_LIMIT),
        name="conformer_pre",
    )(u, za, cw, ln_w.reshape(1, c), ln_b.reshape(1, c))


SSD_ROWS = 2 * CHUNK


def _short_conv_silu(buf_ref, w_ref, b_ref, g, row0):
    w = w_ref[g]
    bias = b_ref[g]
    cols = []
    for c in range(buf_ref.shape[1]):
        lanes = slice(c * LANES, (c + 1) * LANES)
        acc = jnp.broadcast_to(bias[:, lanes], (CHUNK, LANES))
        for k in range(SSM_CONV):
            start = row0 + TAIL_ROWS - (SSM_CONV - 1) + k
            acc = acc + w[k:k + 1, lanes] * buf_ref[g, c, pl.ds(start, CHUNK), :]
        cols.append(acc)
    return _silu(jnp.concatenate(cols, axis=1))


def _ssd_kernel(x_ref, b_ref, c_ref, zs_ref, col_ref, et_ref, rowT_ref, expand_ref,
                wx_ref, bx_ref, wb_ref, bb_ref, wc_ref, bc_ref, dsk_ref, nw_ref, wout_ref, gate_ref,
                o_ref,
                state_ref, xbuf, bbuf, cbuf, ynew_ref, yprev_ref, *, steps_per_seq):
    L = CHUNK
    hpg = HEADS_PER_GROUP
    t = pl.program_id(0)

    @pl.when(t % steps_per_seq == 0)
    def _():
        state_ref[...] = jnp.zeros(state_ref.shape, F32)
        for buf in (xbuf, bbuf, cbuf):
            buf[:, :, 0:TAIL_ROWS, :] = jnp.zeros(buf.shape[:2] + (TAIL_ROWS, LANES), F32)

    @pl.when(t == 0)
    def _():
        ynew_ref[...] = jnp.zeros(ynew_ref.shape, ynew_ref.dtype)

    yprev_ref[...] = ynew_ref[...]
    slice_w = o_ref.shape[1] // N_GROUPS

    def out_proj_slice(i):
        cols = slice(i * slice_w, (i + 1) * slice_w)
        proj = _dot(yprev_ref[...], wout_ref[:, cols])
        o_ref[:, cols] = (gate_ref[:, cols].astype(F32) * proj).astype(o_ref.dtype)

    for g in range(N_GROUPS):
        for src, buf in ((x_ref, xbuf), (b_ref, bbuf), (c_ref, cbuf)):
            for c in range(buf.shape[1]):
                buf[g, c, TAIL_ROWS:TAIL_ROWS + SSD_ROWS, :] = src[g, :, c * LANES:(c + 1) * LANES].astype(F32)

    row = lax.broadcasted_iota(jnp.int32, (L, L), 0)
    col = lax.broadcasted_iota(jnp.int32, (L, L), 1)
    causal = row >= col
    low_half = col < HEADDIM

    for g in range(N_GROUPS):
        out_proj_slice(g)
        et_all = _dot(et_ref[...], expand_ref[g])
        for ci in range(SSD_ROWS // L):
            row0 = ci * L
            rows = slice(row0, row0 + L)
            colv = col_ref[rows, :]
            xs = _short_conv_silu(xbuf, wx_ref, bx_ref, g, row0)
            bm = _short_conv_silu(bbuf, wb_ref, bb_ref, g, row0)
            cm = _short_conv_silu(cbuf, wc_ref, bc_ref, g, row0)
            bt = bm.T.astype(BF16)
            cm16 = cm.astype(BF16)
            cb16 = _dot(cm16, bt).astype(BF16)
            st = state_ref[g]
            y_state = _dot(cm16, st.astype(BF16))
            ecol = et_all[rows, :GROUP_W]
            tcol = et_all[rows, GROUP_W:]
            ys = []
            for q in range(hpg // 2):
                xp = xs[:, q * LANES:(q + 1) * LANES]
                ms = []
                for e in range(2):
                    h = g * hpg + 2 * q + e
                    seg = jnp.broadcast_to(colv[:, h:h + 1], (L, L)) - rowT_ref[h:h + 1, rows]
                    decay = jnp.exp2(jnp.where(causal, seg, -jnp.inf))
                    ms.append(cb16 * decay.astype(BF16))
                lhs = jnp.concatenate(ms, axis=1)
                rhs = jnp.concatenate([jnp.where(low_half, xp, 0.0), jnp.where(low_half, 0.0, xp)],
                                      axis=0).astype(BF16)
                ys.append(_dot(lhs, rhs))
            y = jnp.concatenate(ys, axis=1) + y_state * ecol
            state_ref[g] = st * ecol[L - 1:L, :] + _dot(bt, (xs * tcol).astype(BF16))
            y = y + dsk_ref[g] * xs
            y = y * zs_ref[g, rows, :].astype(F32)
            y = y * lax.rsqrt(jnp.mean(y * y, axis=-1, keepdims=True) + NORM_EPS)
            ynew_ref[rows, g * GROUP_W:(g + 1) * GROUP_W] = (y * nw_ref[g]).astype(ynew_ref.dtype)

    for buf in (xbuf, bbuf, cbuf):
        buf[:, :, 0:TAIL_ROWS, :] = buf[:, :, SSD_ROWS:SSD_ROWS + TAIL_ROWS, :]


def _head_expand_matrix():
    src = jnp.arange(4 * N_HEADS)
    head, is_t = src % N_HEADS, (src // N_HEADS) % 2
    dst = jnp.arange(2 * GROUP_W)
    dst_t, dst_head_in_group = dst // GROUP_W, (dst % GROUP_W) // HEADDIM
    g = jnp.arange(N_GROUPS)[:, None, None]
    hit = (head[None, :, None] == g * HEADS_PER_GROUP + dst_head_in_group[None, None, :]) \
        & (is_t[None, :, None] == dst_t[None, None, :])
    return hit.astype(BF16)


def _ssd_branch(xs, bm, cm, zs, colv, et, rowT, conv_w, conv_b, d_skip, norm_w, w_out, gates, gate_col, seq):
    g, m, gw = xs.shape
    n = bm.shape[2]
    nh = colv.shape[1]
    d_in, d_out = w_out.shape
    nblocks = m // SSD_ROWS
    last = nblocks - 1

    def cur(t):
        return jnp.minimum(t, last)

    def prev(t):
        return jnp.maximum(t - 1, 0)

    def grp(width):
        return pl.BlockSpec((g, SSD_ROWS, width), lambda t: (0, cur(t), 0))

    def const(shape):
        return pl.BlockSpec(shape, lambda t: (0,) * len(shape), pipeline_mode=pl.Buffered(1))

    def group_major(v, width):
        return v.reshape(v.shape[0], g, width).transpose(1, 0, 2)

    wx = group_major(conv_w[:, :D_INNER], gw)
    wb = group_major(conv_w[:, D_INNER:D_INNER + g * n], n)
    wc = group_major(conv_w[:, D_INNER + g * n:], n)
    cbias = conv_b.reshape(1, -1)
    bx = group_major(cbias[:, :D_INNER], gw)
    bb = group_major(cbias[:, D_INNER:D_INNER + g * n], n)
    bc = group_major(cbias[:, D_INNER + g * n:], n)
    dsk = jnp.repeat(d_skip, HEADDIM).reshape(g, 1, gw)
    nw = norm_w.reshape(g, 1, gw)
    return pl.pallas_call(
        functools.partial(_ssd_kernel, steps_per_seq=seq // SSD_ROWS),
        out_shape=jax.ShapeDtypeStruct((m, d_out), BF16),
        grid=(nblocks + 1,),
        in_specs=[grp(gw), grp(n), grp(n), grp(gw),
                  pl.BlockSpec((SSD_ROWS, nh), lambda t: (cur(t), 0)),
                  pl.BlockSpec((SSD_ROWS, 4 * nh), lambda t: (cur(t), 0)),
                  pl.BlockSpec((nh, SSD_ROWS), lambda t: (0, cur(t))),
                  const((g, 4 * nh, 2 * gw)),
                  const((g, SSM_CONV, gw)), const((g, 1, gw)),
                  const((g, SSM_CONV, n)), const((g, 1, n)),
                  const((g, SSM_CONV, n)), const((g, 1, n)),
                  const((g, 1, gw)), const((g, 1, gw)),
                  const((d_in, d_out)),
                  pl.BlockSpec((SSD_ROWS, d_out), lambda t: (prev(t), gate_col))],
        out_specs=pl.BlockSpec((SSD_ROWS, d_out), lambda t: (prev(t), 0)),
        scratch_shapes=[pltpu.VMEM((g, n, gw), F32),
                        pltpu.VMEM((g, gw // LANES, TAIL_ROWS + SSD_ROWS, LANES), F32),
                        pltpu.VMEM((g, n // LANES, TAIL_ROWS + SSD_ROWS, LANES), F32),
                        pltpu.VMEM((g, n // LANES, TAIL_ROWS + SSD_ROWS, LANES), F32),
                        pltpu.VMEM((SSD_ROWS, d_in), BF16),
                        pltpu.VMEM((SSD_ROWS, d_in), BF16)],
        compiler_params=pltpu.CompilerParams(dimension_semantics=("arbitrary",),
                                             vmem_limit_bytes=VMEM_LIMIT),
        name="ssd_branch",
    )(xs, bm, cm, zs, colv, et, rowT, _head_expand_matrix(), wx, bx, wb, bb, wc, bc, dsk, nw, w_out,
      gates)


def _final_kernel(a_ref, wa_ref, ga_ref, yb_ref, wo_ref, x_ref, nw_ref, o_ref):
    merged = ga_ref[...].astype(F32) * _dot(a_ref[...], wa_ref[...]) + yb_ref[...].astype(F32)
    x = x_ref[...] + _dot(merged.astype(BF16), wo_ref[...])
    ms = jnp.mean(x * x, axis=-1, keepdims=True)
    o_ref[...] = x * lax.rsqrt(ms + NORM_EPS) * nw_ref[...]


def _final(act_a, w_a, gates, y_b, w_o, x2, norm_w, tm=256):
    m, d = x2.shape
    rows = pl.BlockSpec((tm, d), lambda i: (i, 0))
    weight = pl.BlockSpec((d, d), lambda i: (0, 0), pipeline_mode=pl.Buffered(1))
    return pl.pallas_call(
        _final_kernel,
        out_shape=jax.ShapeDtypeStruct((m, d), F32),
        grid=(m // tm,),
        in_specs=[rows, weight, rows, rows, weight, rows,
                  pl.BlockSpec((1, d), lambda i: (0, 0))],
        out_specs=rows,
        compiler_params=pltpu.CompilerParams(dimension_semantics=("arbitrary",),
                                             vmem_limit_bytes=VMEM_LIMIT),
        name="final_norm",
    )(act_a, w_a, gates, y_b, w_o, x2, norm_w.reshape(1, d))


def _layer(x2, batch, seq, norm_w, w_in, conv_a_w, conv_a_b, ln_a_w, ln_a_b, w_a_out,
           conv_b_w, conv_b_b, dt_bias, a_log, d_skip, ssm_norm_w, w_b_out):
    w_in_t = w_in.T
    h = _rmsnorm_cast(x2, norm_w)
    u = _proj(h, w_in_t, OFF_A_VAL, CONV_CH, "glu", gate_col0=OFF_A_GATE, tn=512)
    za = _proj(h, w_in_t, OFF_A_Z, CONV_CH, "silu")
    zs = _proj(h, w_in_t, OFF_S_Z, D_INNER, "silu", split_w=GROUP_W)
    xs = _proj(h, w_in_t, OFF_S_X, D_INNER, "id", split_w=GROUP_W)
    bm = _proj(h, w_in_t, OFF_S_B, N_GROUPS * D_STATE, "id", split_w=D_STATE)
    cm = _proj(h, w_in_t, OFF_S_C, N_GROUPS * D_STATE, "id", split_w=D_STATE)
    gates = _proj(h, w_in_t, OFF_G, 2 * D_MODEL, "sigmoid")
    colv, et, rowT = _ssd_scalars(h, w_in_t, OFF_S_DT, dt_bias, a_log)

    act_a = _conformer_pre(u, za, conv_a_w, conv_a_b, ln_a_w, ln_a_b, batch, seq)
    y_b = _ssd_branch(xs, bm, cm, zs, colv, et, rowT, conv_b_w, conv_b_b, d_skip, ssm_norm_w,
                      w_b_out.astype(BF16), gates, 1, seq)
    return act_a, gates, y_b


def kernel(x, norm_w, w_in, conv_a_w, conv_a_b, ln_a_w, ln_a_b, w_a_out, conv_b_w, conv_b_b, dt_bias, a_log,
           d_skip, ssm_norm_w, w_b_out, w_o, final_norm_w):
    batch, seq, d = x.shape
    depth = norm_w.shape[0]
    assert depth == 1, "the residual add is fused with the final norm for a single layer"
    x2 = x.reshape(batch * seq, d)
    act_a, gates, y_b = _layer(x2, batch, seq, norm_w[0], w_in[0], conv_a_w[0], conv_a_b[0], ln_a_w[0],
                               ln_a_b[0], w_a_out[0], conv_b_w[0], conv_b_b[0], dt_bias[0], a_log[0],
                               d_skip[0], ssm_norm_w[0], w_b_out[0])
    out = _final(act_a, w_a_out[0].astype(BF16), gates, y_b, w_o[0].astype(BF16), x2, final_norm_w)
    return out.reshape(batch, seq, d)
```

```python
import functools
import math

import jax
import jax.numpy as jnp
from jax import lax
from jax.experimental import pallas as pl
from jax.experimental.pallas import tpu as pltpu

F32 = jnp.float32
BF16 = jnp.bfloat16

D_MODEL = 2048
CONV_CH = D_MODEL
CONV_WIDTH = 31
D_INNER = 2 * D_MODEL
HEADDIM = 64
N_HEADS = D_INNER // HEADDIM
D_STATE = 128
N_GROUPS = 8
HEADS_PER_GROUP = N_HEADS // N_GROUPS
GROUP_W = HEADS_PER_GROUP * HEADDIM
SSM_CONV = 4
CHUNK = 128
XBC_DIM = D_INNER + 2 * N_GROUPS * D_STATE
NORM_EPS = 1e-6
LN_EPS = 1e-5
LOG2E = math.log2(math.e)

OFF_A_VAL = 0
OFF_A_GATE = OFF_A_VAL + CONV_CH
OFF_A_Z = OFF_A_GATE + CONV_CH
OFF_S_Z = OFF_A_Z + CONV_CH
OFF_S_X = OFF_S_Z + D_INNER
OFF_S_B = OFF_S_X + D_INNER
OFF_S_C = OFF_S_B + N_GROUPS * D_STATE
OFF_S_DT = OFF_S_C + N_GROUPS * D_STATE
OFF_G = OFF_S_DT + N_HEADS

LANES = 128
CONV_HALO = 32
TAIL_ROWS = 8
VMEM_LIMIT = 56 * 1024 * 1024


def _sigmoid(v):
    return 1.0 / (1.0 + jnp.exp(-v))


def _silu(v):
    return v * _sigmoid(v)


def _softplus(v):
    return jnp.maximum(v, 0.0) + jnp.log1p(jnp.exp(-jnp.abs(v)))


def _split2(v):
    v0 = v.astype(BF16)
    return v0, (v - v0.astype(F32)).astype(BF16)


def _split3(v):
    v0 = v.astype(BF16)
    r = v - v0.astype(F32)
    v1 = r.astype(BF16)
    r = r - v1.astype(F32)
    return v0, v1, r.astype(BF16)


def _dot(a, b):
    return jnp.dot(a, b, preferred_element_type=F32)


def _dot_nt(a, b):
    return lax.dot_general(a, b, (((1,), (1,)), ((), ())), preferred_element_type=F32)


def _ssd_scalars(h, w_dt, bias, a_log, col_ref, et_ref, rowT_ref):
    tm = h.shape[0]
    nh = bias.shape[1]
    dt = _softplus(_dot_nt(h, w_dt) + bias)
    da = dt * (-jnp.exp(a_log))
    log2dt = jnp.log(dt) * LOG2E
    row = lax.broadcasted_iota(jnp.int32, (CHUNK, CHUNK), 0)
    col = lax.broadcasted_iota(jnp.int32, (CHUNK, CHUNK), 1)
    tril = (row >= col).astype(BF16)
    for c in range(tm // CHUNK):
        sl = slice(c * CHUNK, (c + 1) * CHUNK)
        d0, d1, d2 = _split3(da[sl])
        cum = _dot(tril, d0) + _dot(tril, d1) + _dot(tril, d2)
        cum2 = cum * LOG2E
        col_ref[sl, :] = cum2
        e_hi, e_lo = _split2(jnp.exp(cum))
        t_hi, t_lo = _split2(jnp.exp(cum[CHUNK - 1:CHUNK, :] - cum) * dt[sl])
        et_ref[sl, :] = jnp.concatenate([e_hi, t_hi, e_lo, t_lo], axis=1)
        both = jnp.concatenate([cum2 - log2dt[sl], cum2], axis=1)
        rowT_ref[:, sl] = both.T[:nh]


def _rmsnorm_kernel(x_ref, w_ref, wdt_ref, bias_ref, alog_ref, o_ref, col_ref, et_ref, rowT_ref):
    x = x_ref[...]
    ms = jnp.mean(x * x, axis=-1, keepdims=True)
    h = (x * lax.rsqrt(ms + NORM_EPS) * w_ref[...]).astype(o_ref.dtype)
    o_ref[...] = h
    _ssd_scalars(h, wdt_ref[...].astype(BF16), bias_ref[...], alog_ref[...], col_ref, et_ref, rowT_ref)


def _rmsnorm_and_scalars(x2, w, wt, col0, dt_bias, a_log, tm=512):
    m, d = x2.shape
    nh = dt_bias.shape[0]
    assert col0 % nh == 0 and tm % CHUNK == 0

    def vec(n):
        return pl.BlockSpec((1, n), lambda i: (0, 0))

    return pl.pallas_call(
        _rmsnorm_kernel,
        out_shape=(jax.ShapeDtypeStruct((m, d), BF16),
                   jax.ShapeDtypeStruct((m, nh), F32),
                   jax.ShapeDtypeStruct((m, 4 * nh), BF16),
                   jax.ShapeDtypeStruct((nh, m), F32)),
        grid=(m // tm,),
        in_specs=[pl.BlockSpec((tm, d), lambda i: (i, 0)), vec(d),
                  pl.BlockSpec((nh, d), lambda i: (col0 // nh, 0)), vec(nh), vec(nh)],
        out_specs=(pl.BlockSpec((tm, d), lambda i: (i, 0)),
                   pl.BlockSpec((tm, nh), lambda i: (i, 0)),
                   pl.BlockSpec((tm, 4 * nh), lambda i: (i, 0)),
                   pl.BlockSpec((nh, tm), lambda i: (0, i))),
        compiler_params=pltpu.CompilerParams(dimension_semantics=("arbitrary",),
                                             vmem_limit_bytes=VMEM_LIMIT),
        name="rmsnorm_scalars",
    )(x2, w.reshape(1, d), wt, dt_bias.reshape(1, nh), a_log.reshape(1, nh))


def _proj_kernel(*refs, mode, nsplit, shift):
    if mode == "glu":
        h_ref, w_ref, w2_ref, o_ref, w16_ref, w216_ref = refs
    elif shift:
        h_ref, w_ref, w2_ref, o_ref, w16_ref = refs
    else:
        h_ref, w_ref, o_ref, w16_ref = refs

    @pl.when(pl.program_id(1) == 0)
    def _():
        if shift:
            w16_ref[...] = jnp.concatenate([w_ref[shift:, :], w2_ref[...]], axis=0).astype(BF16)
        else:
            w16_ref[...] = w_ref[...].astype(BF16)
        if mode == "glu":
            w216_ref[...] = w2_ref[...].astype(BF16)

    h = h_ref[...]
    acc = _dot_nt(h, w16_ref[...])
    if mode == "glu":
        res = acc * _sigmoid(_dot_nt(h, w216_ref[...]))
    elif mode == "silu":
        res = _silu(acc)
    elif mode == "sigmoid":
        res = _sigmoid(acc)
    else:
        res = acc
    width = res.shape[1] // nsplit
    for s in range(nsplit):
        o_ref[s] = res[:, s * width:(s + 1) * width].astype(o_ref.dtype)


def _proj(h, wt, col0, ncols, mode, *, split_w=None, gate_col0=None, tm=1024, tn=1024):
    m, k = h.shape
    cb0, shift = divmod(col0, tn)
    in_specs = [pl.BlockSpec((tm, k), lambda j, i: (i, 0)),
                pl.BlockSpec((tn, k), lambda j, i: (cb0 + j, 0))]
    args = [h, wt]
    scratch = [pltpu.VMEM((tn, k), BF16)]
    if shift:
        assert mode != "glu" and shift % 8 == 0 and tn % shift == 0
        in_specs.append(pl.BlockSpec((shift, k), lambda j, i: ((cb0 + j + 1) * (tn // shift), 0)))
        args.append(wt)
    if mode == "glu":
        gb0 = gate_col0 // tn
        in_specs.append(pl.BlockSpec((tn, k), lambda j, i: (gb0 + j, 0)))
        args.append(wt)
        scratch.append(pltpu.VMEM((tn, k), BF16))
    if split_w is None:
        nsplit = 1
        out_shape = jax.ShapeDtypeStruct((1, m, ncols), BF16)
        out_spec = pl.BlockSpec((1, tm, tn), lambda j, i: (0, i, j))
    else:
        nsplit = tn // split_w
        out_shape = jax.ShapeDtypeStruct((ncols // split_w, m, split_w), BF16)
        out_spec = pl.BlockSpec((nsplit, tm, split_w), lambda j, i: (j, i, 0))
    out = pl.pallas_call(
        functools.partial(_proj_kernel, mode=mode, nsplit=nsplit, shift=shift),
        out_shape=out_shape,
        grid=(ncols // tn, m // tm),
        in_specs=in_specs,
        out_specs=out_spec,
        scratch_shapes=scratch,
        compiler_params=pltpu.CompilerParams(dimension_semantics=("arbitrary", "arbitrary"),
                                             vmem_limit_bytes=VMEM_LIMIT),
        name="proj_" + mode,
    )(*args)
    return out[0] if split_w is None else out


def _conf_kernel(u_ref, za_ref, cw_ref, lnw_ref, lnb_ref, o_ref, ubuf, cbuf, *, rows, conv_rows, ln_rows):
    nblk = ubuf.shape[0]

    @pl.when(pl.program_id(1) == 0)
    def _():
        ubuf[:, 0:CONV_HALO, :] = jnp.zeros((nblk, CONV_HALO, LANES), F32)

    for j in range(nblk):
        ubuf[j, CONV_HALO:CONV_HALO + rows, :] = u_ref[:, j * LANES:(j + 1) * LANES].astype(F32)

    first = CONV_HALO - (CONV_WIDTH - 1)

    def conv_block(j, carry):
        w = cw_ref[j]
        for r0 in range(0, rows, conv_rows):
            acc = jnp.broadcast_to(w[CONV_WIDTH:CONV_WIDTH + 1, :], (conv_rows, LANES))
            for k in range(CONV_WIDTH):
                acc = acc + w[k:k + 1, :] * ubuf[j, pl.ds(r0 + first + k, conv_rows), :]
            cbuf[j, r0:r0 + conv_rows, :] = acc
        ubuf[j, 0:CONV_HALO, :] = ubuf[j, rows:rows + CONV_HALO, :]
        return carry

    lax.fori_loop(0, nblk, conv_block, 0)

    inv_c = 1.0 / (nblk * LANES)

    def ln_block(i, carry):
        r0 = pl.multiple_of(i * ln_rows, ln_rows)
        rows_i = pl.ds(r0, ln_rows)
        total = cbuf[0, rows_i, :]
        for j in range(1, nblk):
            total = total + cbuf[j, rows_i, :]
        mu = jnp.sum(total, axis=-1, keepdims=True) * inv_c
        sq = jnp.zeros((ln_rows, LANES), F32)
        for j in range(nblk):
            d = cbuf[j, rows_i, :] - mu
            sq = sq + d * d
        rstd = lax.rsqrt(jnp.sum(sq, axis=-1, keepdims=True) * inv_c + LN_EPS)
        for j in range(nblk):
            lanes = slice(j * LANES, (j + 1) * LANES)
            v = (cbuf[j, rows_i, :] - mu) * rstd * lnw_ref[:, lanes] + lnb_ref[:, lanes]
            v = _silu(v) * za_ref[rows_i, lanes].astype(F32)
            o_ref[rows_i, lanes] = v.astype(o_ref.dtype)
        return carry

    lax.fori_loop(0, rows // ln_rows, ln_block, 0, unroll=2)


def _conformer_pre(u, za, conv_w, conv_b, ln_w, ln_b, batch, seq, rows=256, conv_rows=64, ln_rows=64):
    m, c = u.shape
    nblk = c // LANES
    assert conv_w.shape[0] == CONV_WIDTH and CONV_WIDTH + 1 <= CONV_HALO
    cw = jnp.concatenate([conv_w, conv_b.reshape(1, c)], axis=0)
    cw = jnp.pad(cw, ((0, CONV_HALO - CONV_WIDTH - 1), (0, 0))).reshape(CONV_HALO, nblk, LANES).transpose(1, 0, 2)
    nt = seq // rows
    tile = pl.BlockSpec((rows, c), lambda b, t: (b * nt + t, 0))
    vec = pl.BlockSpec((1, c), lambda b, t: (0, 0))
    return pl.pallas_call(
        functools.partial(_conf_kernel, rows=rows, conv_rows=conv_rows, ln_rows=ln_rows),
        out_shape=jax.ShapeDtypeStruct((m, c), BF16),
        grid=(batch, nt),
        in_specs=[tile, tile,
                  pl.BlockSpec((nblk, CONV_HALO, LANES), lambda b, t: (0, 0, 0)),
                  vec, vec],
        out_specs=tile,
        scratch_shapes=[pltpu.VMEM((nblk, CONV_HALO + rows, LANES), F32),
                        pltpu.VMEM((nblk, rows, LANES), F32)],
        compiler_params=pltpu.CompilerParams(dimension_semantics=("arbitrary", "arbitrary"),
                                             vmem_limit_bytes=VMEM_LIMIT),
        name="conformer_pre",
    )(u, za, cw, ln_w.reshape(1, c), ln_b.reshape(1, c))


SSD_ROWS = 2 * CHUNK


def _short_conv_silu(buf_ref, w_ref, b_ref, g, row0):
    w = w_ref[g]
    bias = b_ref[g]
    cols = []
    for c in range(buf_ref.shape[1]):
        lanes = slice(c * LANES, (c + 1) * LANES)
        acc = jnp.broadcast_to(bias[:, lanes], (CHUNK, LANES))
        for k in range(SSM_CONV):
            start = row0 + TAIL_ROWS - (SSM_CONV - 1) + k
            acc = acc + w[k:k + 1, lanes] * buf_ref[g, c, pl.ds(start, CHUNK), :]
        cols.append(acc)
    return _silu(jnp.concatenate(cols, axis=1))


def _ssd_kernel(x_ref, b_ref, c_ref, zs_ref, col_ref, et_ref, rowT_ref, expand_ref,
                wx_ref, bx_ref, wb_ref, bb_ref, wc_ref, bc_ref, dsk_ref, nw_ref, wout_ref, gate_ref,
                o_ref,
                state_ref, xbuf, bbuf, cbuf, ynew_ref, yprev_ref, *, steps_per_seq):
    L = CHUNK
    hpg = HEADS_PER_GROUP
    t = pl.program_id(0)

    @pl.when(t % steps_per_seq == 0)
    def _():
        state_ref[...] = jnp.zeros(state_ref.shape, F32)
        for buf in (xbuf, bbuf, cbuf):
            buf[:, :, 0:TAIL_ROWS, :] = jnp.zeros(buf.shape[:2] + (TAIL_ROWS, LANES), F32)

    @pl.when(t == 0)
    def _():
        ynew_ref[...] = jnp.zeros(ynew_ref.shape, ynew_ref.dtype)

    yprev_ref[...] = ynew_ref[...]
    slice_w = o_ref.shape[1] // N_GROUPS

    def out_proj_slice(i):
        cols = slice(i * slice_w, (i + 1) * slice_w)
        proj = _dot(yprev_ref[...], wout_ref[:, cols])
        o_ref[:, cols] = (gate_ref[:, cols].astype(F32) * proj).astype(o_ref.dtype)

    for g in range(N_GROUPS):
        for src, buf in ((x_ref, xbuf), (b_ref, bbuf), (c_ref, cbuf)):
            for c in range(buf.shape[1]):
                buf[g, c, TAIL_ROWS:TAIL_ROWS + SSD_ROWS, :] = src[g, :, c * LANES:(c + 1) * LANES].astype(F32)

    row = lax.broadcasted_iota(jnp.int32, (L, L), 0)
    col = lax.broadcasted_iota(jnp.int32, (L, L), 1)
    causal = row >= col
    low_half = col < HEADDIM

    for g in range(N_GROUPS):
        out_proj_slice(g)
        et_all = _dot(et_ref[...], expand_ref[g])
        for ci in range(SSD_ROWS // L):
            row0 = ci * L
            rows = slice(row0, row0 + L)
            colv = col_ref[rows, :]
            xs = _short_conv_silu(xbuf, wx_ref, bx_ref, g, row0)
            bm = _short_conv_silu(bbuf, wb_ref, bb_ref, g, row0)
            cm = _short_conv_silu(cbuf, wc_ref, bc_ref, g, row0)
            bt = bm.T.astype(BF16)
            cm16 = cm.astype(BF16)
            cb16 = _dot(cm16, bt).astype(BF16)
            st = state_ref[g]
            y_state = _dot(cm16, st.astype(BF16))
            ecol = et_all[rows, :GROUP_W]
            tcol = et_all[rows, GROUP_W:]
            ys = []
            for q in range(hpg // 2):
                xp = xs[:, q * LANES:(q + 1) * LANES]
                ms = []
                for e in range(2):
                    h = g * hpg + 2 * q + e
                    seg = jnp.broadcast_to(colv[:, h:h + 1], (L, L)) - rowT_ref[h:h + 1, rows]
                    decay = jnp.exp2(jnp.where(causal, seg, -jnp.inf))
                    ms.append(cb16 * decay.astype(BF16))
                lhs = jnp.concatenate(ms, axis=1)
                rhs = jnp.concatenate([jnp.where(low_half, xp, 0.0), jnp.where(low_half, 0.0, xp)],
                                      axis=0).astype(BF16)
                ys.append(_dot(lhs, rhs))
            y = jnp.concatenate(ys, axis=1) + y_state * ecol
            state_ref[g] = st * ecol[L - 1:L, :] + _dot(bt, (xs * tcol).astype(BF16))
            y = y + dsk_ref[g] * xs
            y = y * zs_ref[g, rows, :].astype(F32)
            y = y * lax.rsqrt(jnp.mean(y * y, axis=-1, keepdims=True) + NORM_EPS)
            ynew_ref[rows, g * GROUP_W:(g + 1) * GROUP_W] = (y * nw_ref[g]).astype(ynew_ref.dtype)

    for buf in (xbuf, bbuf, cbuf):
        buf[:, :, 0:TAIL_ROWS, :] = buf[:, :, SSD_ROWS:SSD_ROWS + TAIL_ROWS, :]


def _head_expand_matrix():
    src = jnp.arange(4 * N_HEADS)
    head, is_t = src % N_HEADS, (src // N_HEADS) % 2
    dst = jnp.arange(2 * GROUP_W)
    dst_t, dst_head_in_group = dst // GROUP_W, (dst % GROUP_W) // HEADDIM
    g = jnp.arange(N_GROUPS)[:, None, None]
    hit = (head[None, :, None] == g * HEADS_PER_GROUP + dst_head_in_group[None, None, :]) \
        & (is_t[None, :, None] == dst_t[None, None, :])
    return hit.astype(BF16)


def _ssd_branch(xs, bcm, zs, colv, et, rowT, conv_w, conv_b, d_skip, norm_w, w_out, gates, gate_col, seq):
    g, m, gw = xs.shape
    n = bcm.shape[2]
    nh = colv.shape[1]
    d_in, d_out = w_out.shape
    nblocks = m // SSD_ROWS
    last = nblocks - 1

    def cur(t):
        return jnp.minimum(t, last)

    def prev(t):
        return jnp.maximum(t - 1, 0)

    def grp(width):
        return pl.BlockSpec((g, SSD_ROWS, width), lambda t: (0, cur(t), 0))

    def const(shape):
        return pl.BlockSpec(shape, lambda t: (0,) * len(shape), pipeline_mode=pl.Buffered(1))

    def group_major(v, width):
        return v.reshape(v.shape[0], g, width).transpose(1, 0, 2)

    wx = group_major(conv_w[:, :D_INNER], gw)
    wb = group_major(conv_w[:, D_INNER:D_INNER + g * n], n)
    wc = group_major(conv_w[:, D_INNER + g * n:], n)
    cbias = conv_b.reshape(1, -1)
    bx = group_major(cbias[:, :D_INNER], gw)
    bb = group_major(cbias[:, D_INNER:D_INNER + g * n], n)
    bc = group_major(cbias[:, D_INNER + g * n:], n)
    dsk = jnp.repeat(d_skip, HEADDIM).reshape(g, 1, gw)
    nw = norm_w.reshape(g, 1, gw)
    return pl.pallas_call(
        functools.partial(_ssd_kernel, steps_per_seq=seq // SSD_ROWS),
        out_shape=jax.ShapeDtypeStruct((m, d_out), BF16),
        grid=(nblocks + 1,),
        in_specs=[grp(gw),
                  pl.BlockSpec((g, SSD_ROWS, n), lambda t: (0, cur(t), 0)),
                  pl.BlockSpec((g, SSD_ROWS, n), lambda t: (1, cur(t), 0)),
                  grp(gw),
                  pl.BlockSpec((SSD_ROWS, nh), lambda t: (cur(t), 0)),
                  pl.BlockSpec((SSD_ROWS, 4 * nh), lambda t: (cur(t), 0)),
                  pl.BlockSpec((nh, SSD_ROWS), lambda t: (0, cur(t))),
                  const((g, 4 * nh, 2 * gw)),
                  const((g, SSM_CONV, gw)), const((g, 1, gw)),
                  const((g, SSM_CONV, n)), const((g, 1, n)),
                  const((g, SSM_CONV, n)), const((g, 1, n)),
                  const((g, 1, gw)), const((g, 1, gw)),
                  const((d_in, d_out)),
                  pl.BlockSpec((SSD_ROWS, d_out), lambda t: (prev(t), gate_col))],
        out_specs=pl.BlockSpec((SSD_ROWS, d_out), lambda t: (prev(t), 0)),
        scratch_shapes=[pltpu.VMEM((g, n, gw), F32),
                        pltpu.VMEM((g, gw // LANES, TAIL_ROWS + SSD_ROWS, LANES), F32),
                        pltpu.VMEM((g, n // LANES, TAIL_ROWS + SSD_ROWS, LANES), F32),
                        pltpu.VMEM((g, n // LANES, TAIL_ROWS + SSD_ROWS, LANES), F32),
                        pltpu.VMEM((SSD_ROWS, d_in), BF16),
                        pltpu.VMEM((SSD_ROWS, d_in), BF16)],
        compiler_params=pltpu.CompilerParams(dimension_semantics=("arbitrary",),
                                             vmem_limit_bytes=VMEM_LIMIT),
        name="ssd_branch",
    )(xs, bcm, bcm, zs, colv, et, rowT, _head_expand_matrix(), wx, bx, wb, bb, wc, bc, dsk, nw, w_out,
      gates)


def _final_kernel(a_ref, wa_ref, ga_ref, yb_ref, wo_ref, x_ref, nw_ref, o_ref):
    merged = ga_ref[...].astype(F32) * _dot(a_ref[...], wa_ref[...]) + yb_ref[...].astype(F32)
    x = x_ref[...] + _dot(merged.astype(BF16), wo_ref[...])
    ms = jnp.mean(x * x, axis=-1, keepdims=True)
    o_ref[...] = x * lax.rsqrt(ms + NORM_EPS) * nw_ref[...]


def _final(act_a, w_a, gates, y_b, w_o, x2, norm_w, tm=256):
    m, d = x2.shape
    rows = pl.BlockSpec((tm, d), lambda i: (i, 0))
    weight = pl.BlockSpec((d, d), lambda i: (0, 0), pipeline_mode=pl.Buffered(1))
    return pl.pallas_call(
        _final_kernel,
        out_shape=jax.ShapeDtypeStruct((m, d), F32),
        grid=(m // tm,),
        in_specs=[rows, weight, rows, rows, weight, rows,
                  pl.BlockSpec((1, d), lambda i: (0, 0))],
        out_specs=rows,
        compiler_params=pltpu.CompilerParams(dimension_semantics=("arbitrary",),
                                             vmem_limit_bytes=VMEM_LIMIT),
        name="final_norm",
    )(act_a, w_a, gates, y_b, w_o, x2, norm_w.reshape(1, d))


def _layer(x2, batch, seq, norm_w, w_in, conv_a_w, conv_a_b, ln_a_w, ln_a_b, w_a_out,
           conv_b_w, conv_b_b, dt_bias, a_log, d_skip, ssm_norm_w, w_b_out):
    w_in_t = w_in.T
    h, colv, et, rowT = _rmsnorm_and_scalars(x2, norm_w, w_in_t, OFF_S_DT, dt_bias, a_log)
    u = _proj(h, w_in_t, OFF_A_VAL, CONV_CH, "glu", gate_col0=OFF_A_GATE, tn=512)
    za = _proj(h, w_in_t, OFF_A_Z, CONV_CH, "silu")
    zs = _proj(h, w_in_t, OFF_S_Z, D_INNER, "silu", split_w=GROUP_W)
    xs = _proj(h, w_in_t, OFF_S_X, D_INNER, "id", split_w=GROUP_W)
    assert OFF_S_C == OFF_S_B + N_GROUPS * D_STATE
    bcm = _proj(h, w_in_t, OFF_S_B, 2 * N_GROUPS * D_STATE, "id", split_w=D_STATE)
    gates = _proj(h, w_in_t, OFF_G, 2 * D_MODEL, "sigmoid")

    act_a = _conformer_pre(u, za, conv_a_w, conv_a_b, ln_a_w, ln_a_b, batch, seq)
    y_b = _ssd_branch(xs, bcm, zs, colv, et, rowT, conv_b_w, conv_b_b, d_skip, ssm_norm_w,
                      w_b_out.astype(BF16), gates, 1, seq)
    return act_a, gates, y_b


def kernel(x, norm_w, w_in, conv_a_w, conv_a_b, ln_a_w, ln_a_b, w_a_out, conv_b_w, conv_b_b, dt_bias, a_log,
           d_skip, ssm_norm_w, w_b_out, w_o, final_norm_w):
    batch, seq, d = x.shape
    depth = norm_w.shape[0]
    assert depth == 1, "the residual add is fused with the final norm for a single layer"
    x2 = x.reshape(batch * seq, d)
    act_a, gates, y_b = _layer(x2, batch, seq, norm_w[0], w_in[0], conv_a_w[0], conv_a_b[0], ln_a_w[0],
                               ln_a_b[0], w_a_out[0], conv_b_w[0], conv_b_b[0], dt_bias[0], a_log[0],
                               d_skip[0], ssm_norm_w[0], w_b_out[0])
    out = _final(act_a, w_a_out[0].astype(BF16), gates, y_b, w_o[0].astype(BF16), x2, final_norm_w)
    return out.reshape(batch, seq, d)
```

```python
import functools
import math

import jax
import jax.numpy as jnp
from jax import lax
from jax.experimental import pallas as pl
from jax.experimental.pallas import tpu as pltpu

F32 = jnp.float32
BF16 = jnp.bfloat16

D_MODEL = 2048
CONV_CH = D_MODEL
CONV_WIDTH = 31
D_INNER = 2 * D_MODEL
HEADDIM = 64
N_HEADS = D_INNER // HEADDIM
D_STATE = 128
N_GROUPS = 8
HEADS_PER_GROUP = N_HEADS // N_GROUPS
GROUP_W = HEADS_PER_GROUP * HEADDIM
SSM_CONV = 4
CHUNK = 128
XBC_DIM = D_INNER + 2 * N_GROUPS * D_STATE
NORM_EPS = 1e-6
LN_EPS = 1e-5
LOG2E = math.log2(math.e)

OFF_A_VAL = 0
OFF_A_GATE = OFF_A_VAL + CONV_CH
OFF_A_Z = OFF_A_GATE + CONV_CH
OFF_S_Z = OFF_A_Z + CONV_CH
OFF_S_X = OFF_S_Z + D_INNER
OFF_S_B = OFF_S_X + D_INNER
OFF_S_C = OFF_S_B + N_GROUPS * D_STATE
OFF_S_DT = OFF_S_C + N_GROUPS * D_STATE
OFF_G = OFF_S_DT + N_HEADS

LANES = 128
CONV_HALO = 32
TAIL_ROWS = 8
VMEM_LIMIT = 56 * 1024 * 1024


def _sigmoid(v):
    return 0.5 * jnp.tanh(0.5 * v) + 0.5


def _silu(v):
    t = 0.5 * v
    return t * jnp.tanh(t) + t


def _softplus(v):
    return jnp.maximum(v, 0.0) + jnp.log1p(jnp.exp(-jnp.abs(v)))


def _split2(v):
    v0 = v.astype(BF16)
    return v0, (v - v0.astype(F32)).astype(BF16)


def _split3(v):
    v0 = v.astype(BF16)
    r = v - v0.astype(F32)
    v1 = r.astype(BF16)
    r = r - v1.astype(F32)
    return v0, v1, r.astype(BF16)


def _dot(a, b):
    return jnp.dot(a, b, preferred_element_type=F32)


def _dot_nt(a, b):
    return lax.dot_general(a, b, (((1,), (1,)), ((), ())), preferred_element_type=F32)


def _ssd_scalars(h, w_dt, bias, a_log, col_ref, et_ref, rowT_ref):
    tm = h.shape[0]
    nh = bias.shape[1]
    dt = _softplus(_dot_nt(h, w_dt) + bias)
    da = dt * (-jnp.exp(a_log))
    log2dt = jnp.log(dt) * LOG2E
    row = lax.broadcasted_iota(jnp.int32, (CHUNK, CHUNK), 0)
    col = lax.broadcasted_iota(jnp.int32, (CHUNK, CHUNK), 1)
    tril = (row >= col).astype(BF16)
    for c in range(tm // CHUNK):
        sl = slice(c * CHUNK, (c + 1) * CHUNK)
        d0, d1, d2 = _split3(da[sl])
        cum = _dot(tril, d0) + _dot(tril, d1) + _dot(tril, d2)
        cum2 = cum * LOG2E
        col_ref[sl, :] = cum2
        e_hi, e_lo = _split2(jnp.exp(cum))
        t_hi, t_lo = _split2(jnp.exp(cum[CHUNK - 1:CHUNK, :] - cum) * dt[sl])
        et_ref[sl, :] = jnp.concatenate([e_hi, t_hi, e_lo, t_lo], axis=1)
        both = jnp.concatenate([cum2 - log2dt[sl], cum2], axis=1)
        rowT_ref[:, sl] = both.T[:nh]


def _rmsnorm_kernel(x_ref, w_ref, wdt_ref, bias_ref, alog_ref, o_ref, col_ref, et_ref, rowT_ref):
    x = x_ref[...]
    ms = jnp.mean(x * x, axis=-1, keepdims=True)
    h = (x * lax.rsqrt(ms + NORM_EPS) * w_ref[...]).astype(o_ref.dtype)
    o_ref[...] = h
    _ssd_scalars(h, wdt_ref[...].astype(BF16), bias_ref[...], alog_ref[...], col_ref, et_ref, rowT_ref)


def _rmsnorm_and_scalars(x2, w, wt, col0, dt_bias, a_log, tm=512):
    m, d = x2.shape
    nh = dt_bias.shape[0]
    assert col0 % nh == 0 and tm % CHUNK == 0

    def vec(n):
        return pl.BlockSpec((1, n), lambda i: (0, 0))

    return pl.pallas_call(
        _rmsnorm_kernel,
        out_shape=(jax.ShapeDtypeStruct((m, d), BF16),
                   jax.ShapeDtypeStruct((m, nh), F32),
                   jax.ShapeDtypeStruct((m, 4 * nh), BF16),
                   jax.ShapeDtypeStruct((nh, m), F32)),
        grid=(m // tm,),
        in_specs=[pl.BlockSpec((tm, d), lambda i: (i, 0)), vec(d),
                  pl.BlockSpec((nh, d), lambda i: (col0 // nh, 0)), vec(nh), vec(nh)],
        out_specs=(pl.BlockSpec((tm, d), lambda i: (i, 0)),
                   pl.BlockSpec((tm, nh), lambda i: (i, 0)),
                   pl.BlockSpec((tm, 4 * nh), lambda i: (i, 0)),
                   pl.BlockSpec((nh, tm), lambda i: (0, i))),
        compiler_params=pltpu.CompilerParams(dimension_semantics=("arbitrary",),
                                             vmem_limit_bytes=VMEM_LIMIT),
        name="rmsnorm_scalars",
    )(x2, w.reshape(1, d), wt, dt_bias.reshape(1, nh), a_log.reshape(1, nh))


def _proj_kernel(*refs, mode, nsplit, shift):
    if mode == "glu":
        h_ref, w_ref, w2_ref, o_ref, w16_ref, w216_ref = refs
    elif shift:
        h_ref, w_ref, w2_ref, o_ref, w16_ref = refs
    else:
        h_ref, w_ref, o_ref, w16_ref = refs

    @pl.when(pl.program_id(1) == 0)
    def _():
        if shift:
            w16_ref[...] = jnp.concatenate([w_ref[shift:, :], w2_ref[...]], axis=0).astype(BF16)
        else:
            w16_ref[...] = w_ref[...].astype(BF16)
        if mode == "glu":
            w216_ref[...] = w2_ref[...].astype(BF16)

    h = h_ref[...]
    acc = _dot_nt(h, w16_ref[...])
    if mode == "glu":
        res = acc * _sigmoid(_dot_nt(h, w216_ref[...]))
    elif mode == "silu":
        res = _silu(acc)
    elif mode == "sigmoid":
        res = _sigmoid(acc)
    else:
        res = acc
    width = res.shape[1] // nsplit
    for s in range(nsplit):
        o_ref[s] = res[:, s * width:(s + 1) * width].astype(o_ref.dtype)


def _proj(h, wt, col0, ncols, mode, *, split_w=None, gate_col0=None, tm=1024, tn=1024):
    m, k = h.shape
    cb0, shift = divmod(col0, tn)
    in_specs = [pl.BlockSpec((tm, k), lambda j, i: (i, 0)),
                pl.BlockSpec((tn, k), lambda j, i: (cb0 + j, 0))]
    args = [h, wt]
    scratch = [pltpu.VMEM((tn, k), BF16)]
    if shift:
        assert mode != "glu" and shift % 8 == 0 and tn % shift == 0
        in_specs.append(pl.BlockSpec((shift, k), lambda j, i: ((cb0 + j + 1) * (tn // shift), 0)))
        args.append(wt)
    if mode == "glu":
        gb0 = gate_col0 // tn
        in_specs.append(pl.BlockSpec((tn, k), lambda j, i: (gb0 + j, 0)))
        args.append(wt)
        scratch.append(pltpu.VMEM((tn, k), BF16))
    if split_w is None:
        nsplit = 1
        out_shape = jax.ShapeDtypeStruct((1, m, ncols), BF16)
        out_spec = pl.BlockSpec((1, tm, tn), lambda j, i: (0, i, j))
    else:
        nsplit = tn // split_w
        out_shape = jax.ShapeDtypeStruct((ncols // split_w, m, split_w), BF16)
        out_spec = pl.BlockSpec((nsplit, tm, split_w), lambda j, i: (j, i, 0))
    out = pl.pallas_call(
        functools.partial(_proj_kernel, mode=mode, nsplit=nsplit, shift=shift),
        out_shape=out_shape,
        grid=(ncols // tn, m // tm),
        in_specs=in_specs,
        out_specs=out_spec,
        scratch_shapes=scratch,
        compiler_params=pltpu.CompilerParams(dimension_semantics=("arbitrary", "arbitrary"),
                                             vmem_limit_bytes=VMEM_LIMIT),
        name="proj_" + mode,
    )(*args)
    return out[0] if split_w is None else out


def _conf_kernel(u_ref, za_ref, cw_ref, lnw_ref, lnb_ref, o_ref, ubuf, cbuf, *, rows, conv_rows, ln_rows):
    nblk = ubuf.shape[0]

    @pl.when(pl.program_id(1) == 0)
    def _():
        ubuf[:, 0:CONV_HALO, :] = jnp.zeros((nblk, CONV_HALO, LANES), F32)

    for j in range(nblk):
        ubuf[j, CONV_HALO:CONV_HALO + rows, :] = u_ref[:, j * LANES:(j + 1) * LANES].astype(F32)

    first = CONV_HALO - (CONV_WIDTH - 1)

    def conv_block(j, carry):
        w = cw_ref[j]
        for r0 in range(0, rows, conv_rows):
            acc = jnp.broadcast_to(w[CONV_WIDTH:CONV_WIDTH + 1, :], (conv_rows, LANES))
            for k in range(CONV_WIDTH):
                acc = acc + w[k:k + 1, :] * ubuf[j, pl.ds(r0 + first + k, conv_rows), :]
            cbuf[j, r0:r0 + conv_rows, :] = acc
        ubuf[j, 0:CONV_HALO, :] = ubuf[j, rows:rows + CONV_HALO, :]
        return carry

    lax.fori_loop(0, nblk, conv_block, 0)

    inv_c = 1.0 / (nblk * LANES)

    def ln_block(i, carry):
        r0 = pl.multiple_of(i * ln_rows, ln_rows)
        rows_i = pl.ds(r0, ln_rows)
        total = cbuf[0, rows_i, :]
        for j in range(1, nblk):
            total = total + cbuf[j, rows_i, :]
        mu = jnp.sum(total, axis=-1, keepdims=True) * inv_c
        sq = jnp.zeros((ln_rows, LANES), F32)
        for j in range(nblk):
            d = cbuf[j, rows_i, :] - mu
            sq = sq + d * d
        rstd = lax.rsqrt(jnp.sum(sq, axis=-1, keepdims=True) * inv_c + LN_EPS)
        for j in range(nblk):
            lanes = slice(j * LANES, (j + 1) * LANES)
            v = (cbuf[j, rows_i, :] - mu) * rstd * lnw_ref[:, lanes] + lnb_ref[:, lanes]
            v = _silu(v) * za_ref[rows_i, lanes].astype(F32)
            o_ref[rows_i, lanes] = v.astype(o_ref.dtype)
        return carry

    lax.fori_loop(0, rows // ln_rows, ln_block, 0, unroll=2)


def _conformer_pre(u, za, conv_w, conv_b, ln_w, ln_b, batch, seq, rows=256, conv_rows=64, ln_rows=64):
    m, c = u.shape
    nblk = c // LANES
    assert conv_w.shape[0] == CONV_WIDTH and CONV_WIDTH + 1 <= CONV_HALO
    cw = jnp.concatenate([conv_w, conv_b.reshape(1, c)], axis=0)
    cw = jnp.pad(cw, ((0, CONV_HALO - CONV_WIDTH - 1), (0, 0))).reshape(CONV_HALO, nblk, LANES).transpose(1, 0, 2)
    nt = seq // rows
    tile = pl.BlockSpec((rows, c), lambda b, t: (b * nt + t, 0))
    vec = pl.BlockSpec((1, c), lambda b, t: (0, 0))
    return pl.pallas_call(
        functools.partial(_conf_kernel, rows=rows, conv_rows=conv_rows, ln_rows=ln_rows),
        out_shape=jax.ShapeDtypeStruct((m, c), BF16),
        grid=(batch, nt),
        in_specs=[tile, tile,
                  pl.BlockSpec((nblk, CONV_HALO, LANES), lambda b, t: (0, 0, 0)),
                  vec, vec],
        out_specs=tile,
        scratch_shapes=[pltpu.VMEM((nblk, CONV_HALO + rows, LANES), F32),
                        pltpu.VMEM((nblk, rows, LANES), F32)],
        compiler_params=pltpu.CompilerParams(dimension_semantics=("arbitrary", "arbitrary"),
                                             vmem_limit_bytes=VMEM_LIMIT),
        name="conformer_pre",
    )(u, za, cw, ln_w.reshape(1, c), ln_b.reshape(1, c))


SSD_ROWS = 2 * CHUNK


def _short_conv_silu(buf_ref, w_ref, b_ref, g, row0):
    w = w_ref[g]
    bias = b_ref[g]
    cols = []
    for c in range(buf_ref.shape[1]):
        lanes = slice(c * LANES, (c + 1) * LANES)
        acc = jnp.broadcast_to(bias[:, lanes], (CHUNK, LANES))
        for k in range(SSM_CONV):
            start = row0 + TAIL_ROWS - (SSM_CONV - 1) + k
            acc = acc + w[k:k + 1, lanes] * buf_ref[g, c, pl.ds(start, CHUNK), :]
        cols.append(acc)
    return _silu(jnp.concatenate(cols, axis=1))


def _ssd_kernel(x_ref, b_ref, c_ref, zs_ref, col_ref, et_ref, rowT_ref, expand_ref,
                wx_ref, bx_ref, wb_ref, bb_ref, wc_ref, bc_ref, dsk_ref, nw_ref, wout_ref, gate_ref,
                o_ref,
                state_ref, xbuf, bbuf, cbuf, ynew_ref, yprev_ref, *, steps_per_seq):
    L = CHUNK
    hpg = HEADS_PER_GROUP
    t = pl.program_id(0)

    @pl.when(t % steps_per_seq == 0)
    def _():
        state_ref[...] = jnp.zeros(state_ref.shape, F32)
        for buf in (xbuf, bbuf, cbuf):
            buf[:, :, 0:TAIL_ROWS, :] = jnp.zeros(buf.shape[:2] + (TAIL_ROWS, LANES), F32)

    @pl.when(t == 0)
    def _():
        ynew_ref[...] = jnp.zeros(ynew_ref.shape, ynew_ref.dtype)

    yprev_ref[...] = ynew_ref[...]
    slice_w = o_ref.shape[1] // N_GROUPS

    def out_proj_slice(i):
        cols = slice(i * slice_w, (i + 1) * slice_w)
        proj = _dot(yprev_ref[...], wout_ref[:, cols])
        o_ref[:, cols] = (gate_ref[:, cols].astype(F32) * proj).astype(o_ref.dtype)

    for g in range(N_GROUPS):
        for src, buf in ((x_ref, xbuf), (b_ref, bbuf), (c_ref, cbuf)):
            for c in range(buf.shape[1]):
                buf[g, c, TAIL_ROWS:TAIL_ROWS + SSD_ROWS, :] = src[g, :, c * LANES:(c + 1) * LANES].astype(F32)

    row = lax.broadcasted_iota(jnp.int32, (L, L), 0)
    col = lax.broadcasted_iota(jnp.int32, (L, L), 1)
    causal = row >= col
    low_half = col < HEADDIM

    for g in range(N_GROUPS):
        out_proj_slice(g)
        et_all = _dot(et_ref[...], expand_ref[g])
        for ci in range(SSD_ROWS // L):
            row0 = ci * L
            rows = slice(row0, row0 + L)
            colv = col_ref[rows, :]
            xs = _short_conv_silu(xbuf, wx_ref, bx_ref, g, row0)
            bm = _short_conv_silu(bbuf, wb_ref, bb_ref, g, row0)
            cm = _short_conv_silu(cbuf, wc_ref, bc_ref, g, row0)
            bt = bm.T.astype(BF16)
            cm16 = cm.astype(BF16)
            cb16 = _dot(cm16, bt).astype(BF16)
            st = state_ref[g]
            y_state = _dot(cm16, st.astype(BF16))
            ecol = et_all[rows, :GROUP_W]
            tcol = et_all[rows, GROUP_W:]
            ys = []
            for q in range(hpg // 2):
                xp = xs[:, q * LANES:(q + 1) * LANES]
                ms = []
                for e in range(2):
                    h = g * hpg + 2 * q + e
                    seg = jnp.broadcast_to(colv[:, h:h + 1], (L, L)) - rowT_ref[h:h + 1, rows]
                    decay = jnp.exp2(jnp.where(causal, seg, -jnp.inf))
                    ms.append(cb16 * decay.astype(BF16))
                lhs = jnp.concatenate(ms, axis=1)
                rhs = jnp.concatenate([jnp.where(low_half, xp, 0.0), jnp.where(low_half, 0.0, xp)],
                                      axis=0).astype(BF16)
                ys.append(_dot(lhs, rhs))
            y = jnp.concatenate(ys, axis=1) + y_state * ecol
            state_ref[g] = st * ecol[L - 1:L, :] + _dot(bt, (xs * tcol).astype(BF16))
            y = y + dsk_ref[g] * xs
            y = y * zs_ref[g, rows, :].astype(F32)
            y = y * lax.rsqrt(jnp.mean(y * y, axis=-1, keepdims=True) + NORM_EPS)
            ynew_ref[rows, g * GROUP_W:(g + 1) * GROUP_W] = (y * nw_ref[g]).astype(ynew_ref.dtype)

    for buf in (xbuf, bbuf, cbuf):
        buf[:, :, 0:TAIL_ROWS, :] = buf[:, :, SSD_ROWS:SSD_ROWS + TAIL_ROWS, :]


def _head_expand_matrix():
    src = jnp.arange(4 * N_HEADS)
    head, is_t = src % N_HEADS, (src // N_HEADS) % 2
    dst = jnp.arange(2 * GROUP_W)
    dst_t, dst_head_in_group = dst // GROUP_W, (dst % GROUP_W) // HEADDIM
    g = jnp.arange(N_GROUPS)[:, None, None]
    hit = (head[None, :, None] == g * HEADS_PER_GROUP + dst_head_in_group[None, None, :]) \
        & (is_t[None, :, None] == dst_t[None, None, :])
    return hit.astype(BF16)


def _ssd_branch(xs, bcm, zs, colv, et, rowT, conv_w, conv_b, d_skip, norm_w, w_out, gates, gate_col, seq):
    g, m, gw = xs.shape
    n = bcm.shape[2]
    nh = colv.shape[1]
    d_in, d_out = w_out.shape
    nblocks = m // SSD_ROWS
    last = nblocks - 1

    def cur(t):
        return jnp.minimum(t, last)

    def prev(t):
        return jnp.maximum(t - 1, 0)

    def grp(width):
        return pl.BlockSpec((g, SSD_ROWS, width), lambda t: (0, cur(t), 0))

    def const(shape):
        return pl.BlockSpec(shape, lambda t: (0,) * len(shape), pipeline_mode=pl.Buffered(1))

    def group_major(v, width):
        return v.reshape(v.shape[0], g, width).transpose(1, 0, 2)

    wx = group_major(conv_w[:, :D_INNER], gw)
    wb = group_major(conv_w[:, D_INNER:D_INNER + g * n], n)
    wc = group_major(conv_w[:, D_INNER + g * n:], n)
    cbias = conv_b.reshape(1, -1)
    bx = group_major(cbias[:, :D_INNER], gw)
    bb = group_major(cbias[:, D_INNER:D_INNER + g * n], n)
    bc = group_major(cbias[:, D_INNER + g * n:], n)
    dsk = jnp.repeat(d_skip, HEADDIM).reshape(g, 1, gw)
    nw = norm_w.reshape(g, 1, gw)
    return pl.pallas_call(
        functools.partial(_ssd_kernel, steps_per_seq=seq // SSD_ROWS),
        out_shape=jax.ShapeDtypeStruct((m, d_out), BF16),
        grid=(nblocks + 1,),
        in_specs=[grp(gw),
                  pl.BlockSpec((g, SSD_ROWS, n), lambda t: (0, cur(t), 0)),
                  pl.BlockSpec((g, SSD_ROWS, n), lambda t: (1, cur(t), 0)),
                  grp(gw),
                  pl.BlockSpec((SSD_ROWS, nh), lambda t: (cur(t), 0)),
                  pl.BlockSpec((SSD_ROWS, 4 * nh), lambda t: (cur(t), 0)),
                  pl.BlockSpec((nh, SSD_ROWS), lambda t: (0, cur(t))),
                  const((g, 4 * nh, 2 * gw)),
                  const((g, SSM_CONV, gw)), const((g, 1, gw)),
                  const((g, SSM_CONV, n)), const((g, 1, n)),
                  const((g, SSM_CONV, n)), const((g, 1, n)),
                  const((g, 1, gw)), const((g, 1, gw)),
                  const((d_in, d_out)),
                  pl.BlockSpec((SSD_ROWS, d_out), lambda t: (prev(t), gate_col))],
        out_specs=pl.BlockSpec((SSD_ROWS, d_out), lambda t: (prev(t), 0)),
        scratch_shapes=[pltpu.VMEM((g, n, gw), F32),
                        pltpu.VMEM((g, gw // LANES, TAIL_ROWS + SSD_ROWS, LANES), F32),
                        pltpu.VMEM((g, n // LANES, TAIL_ROWS + SSD_ROWS, LANES), F32),
                        pltpu.VMEM((g, n // LANES, TAIL_ROWS + SSD_ROWS, LANES), F32),
                        pltpu.VMEM((SSD_ROWS, d_in), BF16),
                        pltpu.VMEM((SSD_ROWS, d_in), BF16)],
        compiler_params=pltpu.CompilerParams(dimension_semantics=("arbitrary",),
                                             vmem_limit_bytes=VMEM_LIMIT),
        name="ssd_branch",
    )(xs, bcm, bcm, zs, colv, et, rowT, _head_expand_matrix(), wx, bx, wb, bb, wc, bc, dsk, nw, w_out,
      gates)


def _final_kernel(a_ref, wa_ref, ga_ref, yb_ref, wo_ref, x_ref, nw_ref, o_ref):
    merged = ga_ref[...].astype(F32) * _dot(a_ref[...], wa_ref[...]) + yb_ref[...].astype(F32)
    x = x_ref[...] + _dot(merged.astype(BF16), wo_ref[...])
    ms = jnp.mean(x * x, axis=-1, keepdims=True)
    o_ref[...] = x * lax.rsqrt(ms + NORM_EPS) * nw_ref[...]


def _final(act_a, w_a, gates, y_b, w_o, x2, norm_w, tm=256):
    m, d = x2.shape
    rows = pl.BlockSpec((tm, d), lambda i: (i, 0))
    weight = pl.BlockSpec((d, d), lambda i: (0, 0), pipeline_mode=pl.Buffered(1))
    return pl.pallas_call(
        _final_kernel,
        out_shape=jax.ShapeDtypeStruct((m, d), F32),
        grid=(m // tm,),
        in_specs=[rows, weight, rows, rows, weight, rows,
                  pl.BlockSpec((1, d), lambda i: (0, 0))],
        out_specs=rows,
        compiler_params=pltpu.CompilerParams(dimension_semantics=("arbitrary",),
                                             vmem_limit_bytes=VMEM_LIMIT),
        name="final_norm",
    )(act_a, w_a, gates, y_b, w_o, x2, norm_w.reshape(1, d))


def _layer(x2, batch, seq, norm_w, w_in, conv_a_w, conv_a_b, ln_a_w, ln_a_b, w_a_out,
           conv_b_w, conv_b_b, dt_bias, a_log, d_skip, ssm_norm_w, w_b_out):
    w_in_t = w_in.T
    h, colv, et, rowT = _rmsnorm_and_scalars(x2, norm_w, w_in_t, OFF_S_DT, dt_bias, a_log)
    u = _proj(h, w_in_t, OFF_A_VAL, CONV_CH, "glu", gate_col0=OFF_A_GATE, tn=512)
    za = _proj(h, w_in_t, OFF_A_Z, CONV_CH, "silu")
    zs = _proj(h, w_in_t, OFF_S_Z, D_INNER, "silu", split_w=GROUP_W)
    xs = _proj(h, w_in_t, OFF_S_X, D_INNER, "id", split_w=GROUP_W)
    assert OFF_S_C == OFF_S_B + N_GROUPS * D_STATE
    bcm = _proj(h, w_in_t, OFF_S_B, 2 * N_GROUPS * D_STATE, "id", split_w=D_STATE)
    gates = _proj(h, w_in_t, OFF_G, 2 * D_MODEL, "sigmoid")

    act_a = _conformer_pre(u, za, conv_a_w, conv_a_b, ln_a_w, ln_a_b, batch, seq)
    y_b = _ssd_branch(xs, bcm, zs, colv, et, rowT, conv_b_w, conv_b_b, d_skip, ssm_norm_w,
                      w_b_out.astype(BF16), gates, 1, seq)
    return act_a, gates, y_b


def kernel(x, norm_w, w_in, conv_a_w, conv_a_b, ln_a_w, ln_a_b, w_a_out, conv_b_w, conv_b_b, dt_bias, a_log,
           d_skip, ssm_norm_w, w_b_out, w_o, final_norm_w):
    batch, seq, d = x.shape
    depth = norm_w.shape[0]
    assert depth == 1, "the residual add is fused with the final norm for a single layer"
    x2 = x.reshape(batch * seq, d)
    act_a, gates, y_b = _layer(x2, batch, seq, norm_w[0], w_in[0], conv_a_w[0], conv_a_b[0], ln_a_w[0],
                               ln_a_b[0], w_a_out[0], conv_b_w[0], conv_b_b[0], dt_bias[0], a_log[0],
                               d_skip[0], ssm_norm_w[0], w_b_out[0])
    out = _final(act_a, w_a_out[0].astype(BF16), gates, y_b, w_o[0].astype(BF16), x2, final_norm_w)
    return out.reshape(batch, seq, d)
```

```python
import functools
import math

import jax
import jax.numpy as jnp
from jax import lax
from jax.experimental import pallas as pl
from jax.experimental.pallas import tpu as pltpu

F32 = jnp.float32
BF16 = jnp.bfloat16

D_MODEL = 2048
CONV_CH = D_MODEL
CONV_WIDTH = 31
D_INNER = 2 * D_MODEL
HEADDIM = 64
N_HEADS = D_INNER // HEADDIM
D_STATE = 128
N_GROUPS = 8
HEADS_PER_GROUP = N_HEADS // N_GROUPS
GROUP_W = HEADS_PER_GROUP * HEADDIM
SSM_CONV = 4
CHUNK = 128
XBC_DIM = D_INNER + 2 * N_GROUPS * D_STATE
NORM_EPS = 1e-6
LN_EPS = 1e-5
LOG2E = math.log2(math.e)

OFF_A_VAL = 0
OFF_A_GATE = OFF_A_VAL + CONV_CH
OFF_A_Z = OFF_A_GATE + CONV_CH
OFF_S_Z = OFF_A_Z + CONV_CH
OFF_S_X = OFF_S_Z + D_INNER
OFF_S_B = OFF_S_X + D_INNER
OFF_S_C = OFF_S_B + N_GROUPS * D_STATE
OFF_S_DT = OFF_S_C + N_GROUPS * D_STATE
OFF_G = OFF_S_DT + N_HEADS

LANES = 128
CONV_HALO = 32
TAIL_ROWS = 8
VMEM_LIMIT = 56 * 1024 * 1024


def _sigmoid(v):
    return 0.5 * jnp.tanh(0.5 * v) + 0.5


def _silu(v):
    t = 0.5 * v
    return t * jnp.tanh(t) + t


def _softplus(v):
    return jnp.maximum(v, 0.0) + jnp.log1p(jnp.exp(-jnp.abs(v)))


def _split2(v):
    v0 = v.astype(BF16)
    return v0, (v - v0.astype(F32)).astype(BF16)


def _split3(v):
    v0 = v.astype(BF16)
    r = v - v0.astype(F32)
    v1 = r.astype(BF16)
    r = r - v1.astype(F32)
    return v0, v1, r.astype(BF16)


def _dot(a, b):
    return jnp.dot(a, b, preferred_element_type=F32)


def _dot_nt(a, b):
    return lax.dot_general(a, b, (((1,), (1,)), ((), ())), preferred_element_type=F32)


def _ssd_scalars(h, w_dt, bias, a_log, col_ref, et_ref, rowT_ref):
    tm = h.shape[0]
    nh = bias.shape[1]
    dt = _softplus(_dot_nt(h, w_dt) + bias)
    da = dt * (-jnp.exp(a_log))
    log2dt = jnp.log(dt) * LOG2E
    row = lax.broadcasted_iota(jnp.int32, (CHUNK, CHUNK), 0)
    col = lax.broadcasted_iota(jnp.int32, (CHUNK, CHUNK), 1)
    tril = (row >= col).astype(BF16)
    for c in range(tm // CHUNK):
        sl = slice(c * CHUNK, (c + 1) * CHUNK)
        d0, d1, d2 = _split3(da[sl])
        cum = _dot(tril, d0) + _dot(tril, d1) + _dot(tril, d2)
        cum2 = cum * LOG2E
        col_ref[sl, :] = cum2
        e_hi, e_lo = _split2(jnp.exp(cum))
        t_hi, t_lo = _split2(jnp.exp(cum[CHUNK - 1:CHUNK, :] - cum) * dt[sl])
        et_ref[sl, :] = jnp.concatenate([e_hi, t_hi, e_lo, t_lo], axis=1)
        both = jnp.concatenate([cum2 - log2dt[sl], cum2], axis=1)
        rowT_ref[:, sl] = both.T[:nh]


def _rmsnorm_kernel(x_ref, w_ref, wdt_ref, bias_ref, alog_ref, o_ref, col_ref, et_ref, rowT_ref):
    x = x_ref[...]
    ms = jnp.mean(x * x, axis=-1, keepdims=True)
    h = (x * lax.rsqrt(ms + NORM_EPS) * w_ref[...]).astype(o_ref.dtype)
    o_ref[...] = h
    _ssd_scalars(h, wdt_ref[...].astype(BF16), bias_ref[...], alog_ref[...], col_ref, et_ref, rowT_ref)


def _rmsnorm_and_scalars(x2, w, wt, col0, dt_bias, a_log, tm=512):
    m, d = x2.shape
    nh = dt_bias.shape[0]
    assert col0 % nh == 0 and tm % CHUNK == 0

    def vec(n):
        return pl.BlockSpec((1, n), lambda i: (0, 0))

    return pl.pallas_call(
        _rmsnorm_kernel,
        out_shape=(jax.ShapeDtypeStruct((m, d), BF16),
                   jax.ShapeDtypeStruct((m, nh), F32),
                   jax.ShapeDtypeStruct((m, 4 * nh), BF16),
                   jax.ShapeDtypeStruct((nh, m), F32)),
        grid=(m // tm,),
        in_specs=[pl.BlockSpec((tm, d), lambda i: (i, 0)), vec(d),
                  pl.BlockSpec((nh, d), lambda i: (col0 // nh, 0)), vec(nh), vec(nh)],
        out_specs=(pl.BlockSpec((tm, d), lambda i: (i, 0)),
                   pl.BlockSpec((tm, nh), lambda i: (i, 0)),
                   pl.BlockSpec((tm, 4 * nh), lambda i: (i, 0)),
                   pl.BlockSpec((nh, tm), lambda i: (0, i))),
        compiler_params=pltpu.CompilerParams(dimension_semantics=("arbitrary",),
                                             vmem_limit_bytes=VMEM_LIMIT),
        name="rmsnorm_scalars",
    )(x2, w.reshape(1, d), wt, dt_bias.reshape(1, nh), a_log.reshape(1, nh))


def _proj_kernel(*refs, mode, nsplit, shift):
    if mode == "glu":
        h_ref, w_ref, w2_ref, o_ref, w16_ref, w216_ref = refs
    elif shift:
        h_ref, w_ref, w2_ref, o_ref, w16_ref = refs
    else:
        h_ref, w_ref, o_ref, w16_ref = refs

    @pl.when(pl.program_id(1) == 0)
    def _():
        if shift:
            w16_ref[...] = jnp.concatenate([w_ref[shift:, :], w2_ref[...]], axis=0).astype(BF16)
        else:
            w16_ref[...] = w_ref[...].astype(BF16)
        if mode == "glu":
            w216_ref[...] = w2_ref[...].astype(BF16)

    h = h_ref[...]
    acc = _dot_nt(h, w16_ref[...])
    if mode == "glu":
        res = acc * _sigmoid(_dot_nt(h, w216_ref[...]))
    elif mode == "silu":
        res = _silu(acc)
    elif mode == "sigmoid":
        res = _sigmoid(acc)
    else:
        res = acc
    width = res.shape[1] // nsplit
    for s in range(nsplit):
        o_ref[s] = res[:, s * width:(s + 1) * width].astype(o_ref.dtype)


def _proj(h, wt, col0, ncols, mode, *, split_w=None, gate_col0=None, tm=1024, tn=1024):
    m, k = h.shape
    cb0, shift = divmod(col0, tn)
    in_specs = [pl.BlockSpec((tm, k), lambda j, i: (i, 0)),
                pl.BlockSpec((tn, k), lambda j, i: (cb0 + j, 0))]
    args = [h, wt]
    scratch = [pltpu.VMEM((tn, k), BF16)]
    if shift:
        assert mode != "glu" and shift % 8 == 0 and tn % shift == 0
        in_specs.append(pl.BlockSpec((shift, k), lambda j, i: ((cb0 + j + 1) * (tn // shift), 0)))
        args.append(wt)
    if mode == "glu":
        gb0 = gate_col0 // tn
        in_specs.append(pl.BlockSpec((tn, k), lambda j, i: (gb0 + j, 0)))
        args.append(wt)
        scratch.append(pltpu.VMEM((tn, k), BF16))
    if split_w is None:
        nsplit = 1
        out_shape = jax.ShapeDtypeStruct((1, m, ncols), BF16)
        out_spec = pl.BlockSpec((1, tm, tn), lambda j, i: (0, i, j))
    else:
        nsplit = tn // split_w
        out_shape = jax.ShapeDtypeStruct((ncols // split_w, m, split_w), BF16)
        out_spec = pl.BlockSpec((nsplit, tm, split_w), lambda j, i: (j, i, 0))
    out = pl.pallas_call(
        functools.partial(_proj_kernel, mode=mode, nsplit=nsplit, shift=shift),
        out_shape=out_shape,
        grid=(ncols // tn, m // tm),
        in_specs=in_specs,
        out_specs=out_spec,
        scratch_shapes=scratch,
        compiler_params=pltpu.CompilerParams(dimension_semantics=("arbitrary", "arbitrary"),
                                             vmem_limit_bytes=VMEM_LIMIT),
        name="proj_" + mode,
    )(*args)
    return out[0] if split_w is None else out


def _conf_kernel(u_ref, za_ref, cw_ref, lnw_ref, lnb_ref, o_ref, ubuf, cbuf, *, rows, conv_rows, ln_rows):
    nblk = ubuf.shape[0]

    @pl.when(pl.program_id(1) == 0)
    def _():
        ubuf[:, 0:CONV_HALO, :] = jnp.zeros((nblk, CONV_HALO, LANES), F32)

    for j in range(nblk):
        ubuf[j, CONV_HALO:CONV_HALO + rows, :] = u_ref[:, j * LANES:(j + 1) * LANES].astype(F32)

    first = CONV_HALO - (CONV_WIDTH - 1)

    def conv_block(j, carry):
        w = cw_ref[j]
        for r0 in range(0, rows, conv_rows):
            acc = jnp.broadcast_to(w[CONV_WIDTH:CONV_WIDTH + 1, :], (conv_rows, LANES))
            for k in range(CONV_WIDTH):
                acc = acc + w[k:k + 1, :] * ubuf[j, pl.ds(r0 + first + k, conv_rows), :]
            cbuf[j, r0:r0 + conv_rows, :] = acc
        ubuf[j, 0:CONV_HALO, :] = ubuf[j, rows:rows + CONV_HALO, :]
        return carry

    lax.fori_loop(0, nblk, conv_block, 0)

    inv_c = 1.0 / (nblk * LANES)

    def ln_block(i, carry):
        r0 = pl.multiple_of(i * ln_rows, ln_rows)
        rows_i = pl.ds(r0, ln_rows)
        total = cbuf[0, rows_i, :]
        for j in range(1, nblk):
            total = total + cbuf[j, rows_i, :]
        mu = jnp.sum(total, axis=-1, keepdims=True) * inv_c
        sq = jnp.zeros((ln_rows, LANES), F32)
        for j in range(nblk):
            d = cbuf[j, rows_i, :] - mu
            sq = sq + d * d
        rstd = lax.rsqrt(jnp.sum(sq, axis=-1, keepdims=True) * inv_c + LN_EPS)
        for j in range(nblk):
            lanes = slice(j * LANES, (j + 1) * LANES)
            v = (cbuf[j, rows_i, :] - mu) * rstd * lnw_ref[:, lanes] + lnb_ref[:, lanes]
            v = _silu(v) * za_ref[rows_i, lanes].astype(F32)
            o_ref[rows_i, lanes] = v.astype(o_ref.dtype)
        return carry

    lax.fori_loop(0, rows // ln_rows, ln_block, 0, unroll=2)


def _conformer_pre(u, za, conv_w, conv_b, ln_w, ln_b, batch, seq, rows=512, conv_rows=64, ln_rows=64):
    m, c = u.shape
    nblk = c // LANES
    assert conv_w.shape[0] == CONV_WIDTH and CONV_WIDTH + 1 <= CONV_HALO
    cw = jnp.concatenate([conv_w, conv_b.reshape(1, c)], axis=0)
    cw = jnp.pad(cw, ((0, CONV_HALO - CONV_WIDTH - 1), (0, 0))).reshape(CONV_HALO, nblk, LANES).transpose(1, 0, 2)
    nt = seq // rows
    tile = pl.BlockSpec((rows, c), lambda b, t: (b * nt + t, 0))
    vec = pl.BlockSpec((1, c), lambda b, t: (0, 0))
    return pl.pallas_call(
        functools.partial(_conf_kernel, rows=rows, conv_rows=conv_rows, ln_rows=ln_rows),
        out_shape=jax.ShapeDtypeStruct((m, c), BF16),
        grid=(batch, nt),
        in_specs=[tile, tile,
                  pl.BlockSpec((nblk, CONV_HALO, LANES), lambda b, t: (0, 0, 0)),
                  vec, vec],
        out_specs=tile,
        scratch_shapes=[pltpu.VMEM((nblk, CONV_HALO + rows, LANES), F32),
                        pltpu.VMEM((nblk, rows, LANES), F32)],
        compiler_params=pltpu.CompilerParams(dimension_semantics=("arbitrary", "arbitrary"),
                                             vmem_limit_bytes=VMEM_LIMIT),
        name="conformer_pre",
    )(u, za, cw, ln_w.reshape(1, c), ln_b.reshape(1, c))


SSD_ROWS = 2 * CHUNK


def _short_conv_silu(buf_ref, w_ref, b_ref, g, row0):
    w = w_ref[g]
    bias = b_ref[g]
    cols = []
    for c in range(buf_ref.shape[1]):
        lanes = slice(c * LANES, (c + 1) * LANES)
        acc = jnp.broadcast_to(bias[:, lanes], (CHUNK, LANES))
        for k in range(SSM_CONV):
            start = row0 + TAIL_ROWS - (SSM_CONV - 1) + k
            acc = acc + w[k:k + 1, lanes] * buf_ref[g, c, pl.ds(start, CHUNK), :]
        cols.append(acc)
    return _silu(jnp.concatenate(cols, axis=1))


def _ssd_kernel(x_ref, b_ref, c_ref, zs_ref, col_ref, et_ref, rowT_ref, expand_ref,
                wx_ref, bx_ref, wb_ref, bb_ref, wc_ref, bc_ref, dsk_ref, nw_ref, wout_ref, gate_ref,
                o_ref,
                state_ref, xbuf, bbuf, cbuf, ynew_ref, yprev_ref, *, steps_per_seq):
    L = CHUNK
    hpg = HEADS_PER_GROUP
    t = pl.program_id(0)

    @pl.when(t % steps_per_seq == 0)
    def _():
        state_ref[...] = jnp.zeros(state_ref.shape, F32)
        for buf in (xbuf, bbuf, cbuf):
            buf[:, :, 0:TAIL_ROWS, :] = jnp.zeros(buf.shape[:2] + (TAIL_ROWS, LANES), F32)

    @pl.when(t == 0)
    def _():
        ynew_ref[...] = jnp.zeros(ynew_ref.shape, ynew_ref.dtype)

    yprev_ref[...] = ynew_ref[...]
    slice_w = o_ref.shape[1] // N_GROUPS

    def out_proj_slice(i):
        cols = slice(i * slice_w, (i + 1) * slice_w)
        proj = _dot(yprev_ref[...], wout_ref[:, cols])
        o_ref[:, cols] = (gate_ref[:, cols].astype(F32) * proj).astype(o_ref.dtype)

    for g in range(N_GROUPS):
        for src, buf in ((x_ref, xbuf), (b_ref, bbuf), (c_ref, cbuf)):
            for c in range(buf.shape[1]):
                buf[g, c, TAIL_ROWS:TAIL_ROWS + SSD_ROWS, :] = src[g, :, c * LANES:(c + 1) * LANES].astype(F32)

    row = lax.broadcasted_iota(jnp.int32, (L, L), 0)
    col = lax.broadcasted_iota(jnp.int32, (L, L), 1)
    causal = row >= col
    low_half = col < HEADDIM

    for g in range(N_GROUPS):
        out_proj_slice(g)
        et_all = _dot(et_ref[...], expand_ref[g])
        for ci in range(SSD_ROWS // L):
            row0 = ci * L
            rows = slice(row0, row0 + L)
            colv = col_ref[rows, :]
            xs = _short_conv_silu(xbuf, wx_ref, bx_ref, g, row0)
            bm = _short_conv_silu(bbuf, wb_ref, bb_ref, g, row0)
            cm = _short_conv_silu(cbuf, wc_ref, bc_ref, g, row0)
            bt = bm.T.astype(BF16)
            cm16 = cm.astype(BF16)
            cb16 = _dot(cm16, bt).astype(BF16)
            st = state_ref[g]
            y_state = _dot(cm16, st.astype(BF16))
            ecol = et_all[rows, :GROUP_W]
            tcol = et_all[rows, GROUP_W:]
            ys = []
            for q in range(hpg // 2):
                xp = xs[:, q * LANES:(q + 1) * LANES]
                ms = []
                for e in range(2):
                    h = g * hpg + 2 * q + e
                    seg = jnp.broadcast_to(colv[:, h:h + 1], (L, L)) - rowT_ref[h:h + 1, rows]
                    decay = jnp.exp2(jnp.where(causal, seg, -jnp.inf))
                    ms.append(cb16 * decay.astype(BF16))
                lhs = jnp.concatenate(ms, axis=1)
                rhs = jnp.concatenate([jnp.where(low_half, xp, 0.0), jnp.where(low_half, 0.0, xp)],
                                      axis=0).astype(BF16)
                ys.append(_dot(lhs, rhs))
            y = jnp.concatenate(ys, axis=1) + y_state * ecol
            state_ref[g] = st * ecol[L - 1:L, :] + _dot(bt, (xs * tcol).astype(BF16))
            y = y + dsk_ref[g] * xs
            y = y * zs_ref[g, rows, :].astype(F32)
            y = y * lax.rsqrt(jnp.mean(y * y, axis=-1, keepdims=True) + NORM_EPS)
            ynew_ref[rows, g * GROUP_W:(g + 1) * GROUP_W] = (y * nw_ref[g]).astype(ynew_ref.dtype)

    for buf in (xbuf, bbuf, cbuf):
        buf[:, :, 0:TAIL_ROWS, :] = buf[:, :, SSD_ROWS:SSD_ROWS + TAIL_ROWS, :]


def _head_expand_matrix():
    src = jnp.arange(4 * N_HEADS)
    head, is_t = src % N_HEADS, (src // N_HEADS) % 2
    dst = jnp.arange(2 * GROUP_W)
    dst_t, dst_head_in_group = dst // GROUP_W, (dst % GROUP_W) // HEADDIM
    g = jnp.arange(N_GROUPS)[:, None, None]
    hit = (head[None, :, None] == g * HEADS_PER_GROUP + dst_head_in_group[None, None, :]) \
        & (is_t[None, :, None] == dst_t[None, None, :])
    return hit.astype(BF16)


def _ssd_branch(xs, bcm, zs, colv, et, rowT, conv_w, conv_b, d_skip, norm_w, w_out, gates, gate_col, seq):
    g, m, gw = xs.shape
    n = bcm.shape[2]
    nh = colv.shape[1]
    d_in, d_out = w_out.shape
    nblocks = m // SSD_ROWS
    last = nblocks - 1

    def cur(t):
        return jnp.minimum(t, last)

    def prev(t):
        return jnp.maximum(t - 1, 0)

    def grp(width):
        return pl.BlockSpec((g, SSD_ROWS, width), lambda t: (0, cur(t), 0))

    def const(shape):
        return pl.BlockSpec(shape, lambda t: (0,) * len(shape), pipeline_mode=pl.Buffered(1))

    def group_major(v, width):
        return v.reshape(v.shape[0], g, width).transpose(1, 0, 2)

    wx = group_major(conv_w[:, :D_INNER], gw)
    wb = group_major(conv_w[:, D_INNER:D_INNER + g * n], n)
    wc = group_major(conv_w[:, D_INNER + g * n:], n)
    cbias = conv_b.reshape(1, -1)
    bx = group_major(cbias[:, :D_INNER], gw)
    bb = group_major(cbias[:, D_INNER:D_INNER + g * n], n)
    bc = group_major(cbias[:, D_INNER + g * n:], n)
    dsk = jnp.repeat(d_skip, HEADDIM).reshape(g, 1, gw)
    nw = norm_w.reshape(g, 1, gw)
    return pl.pallas_call(
        functools.partial(_ssd_kernel, steps_per_seq=seq // SSD_ROWS),
        out_shape=jax.ShapeDtypeStruct((m, d_out), BF16),
        grid=(nblocks + 1,),
        in_specs=[grp(gw),
                  pl.BlockSpec((g, SSD_ROWS, n), lambda t: (0, cur(t), 0)),
                  pl.BlockSpec((g, SSD_ROWS, n), lambda t: (1, cur(t), 0)),
                  grp(gw),
                  pl.BlockSpec((SSD_ROWS, nh), lambda t: (cur(t), 0)),
                  pl.BlockSpec((SSD_ROWS, 4 * nh), lambda t: (cur(t), 0)),
                  pl.BlockSpec((nh, SSD_ROWS), lambda t: (0, cur(t))),
                  const((g, 4 * nh, 2 * gw)),
                  const((g, SSM_CONV, gw)), const((g, 1, gw)),
                  const((g, SSM_CONV, n)), const((g, 1, n)),
                  const((g, SSM_CONV, n)), const((g, 1, n)),
                  const((g, 1, gw)), const((g, 1, gw)),
                  const((d_in, d_out)),
                  pl.BlockSpec((SSD_ROWS, d_out), lambda t: (prev(t), gate_col))],
        out_specs=pl.BlockSpec((SSD_ROWS, d_out), lambda t: (prev(t), 0)),
        scratch_shapes=[pltpu.VMEM((g, n, gw), F32),
                        pltpu.VMEM((g, gw // LANES, TAIL_ROWS + SSD_ROWS, LANES), F32),
                        pltpu.VMEM((g, n // LANES, TAIL_ROWS + SSD_ROWS, LANES), F32),
                        pltpu.VMEM((g, n // LANES, TAIL_ROWS + SSD_ROWS, LANES), F32),
                        pltpu.VMEM((SSD_ROWS, d_in), BF16),
                        pltpu.VMEM((SSD_ROWS, d_in), BF16)],
        compiler_params=pltpu.CompilerParams(dimension_semantics=("arbitrary",),
                                             vmem_limit_bytes=VMEM_LIMIT),
        name="ssd_branch",
    )(xs, bcm, bcm, zs, colv, et, rowT, _head_expand_matrix(), wx, bx, wb, bb, wc, bc, dsk, nw, w_out,
      gates)


def _final_kernel(a_ref, wa_ref, ga_ref, yb_ref, wo_ref, x_ref, nw_ref, o_ref):
    merged = ga_ref[...].astype(F32) * _dot(a_ref[...], wa_ref[...]) + yb_ref[...].astype(F32)
    x = x_ref[...] + _dot(merged.astype(BF16), wo_ref[...])
    ms = jnp.mean(x * x, axis=-1, keepdims=True)
    o_ref[...] = x * lax.rsqrt(ms + NORM_EPS) * nw_ref[...]


def _final(act_a, w_a, gates, y_b, w_o, x2, norm_w, tm=256):
    m, d = x2.shape
    rows = pl.BlockSpec((tm, d), lambda i: (i, 0))
    weight = pl.BlockSpec((d, d), lambda i: (0, 0), pipeline_mode=pl.Buffered(1))
    return pl.pallas_call(
        _final_kernel,
        out_shape=jax.ShapeDtypeStruct((m, d), F32),
        grid=(m // tm,),
        in_specs=[rows, weight, rows, rows, weight, rows,
                  pl.BlockSpec((1, d), lambda i: (0, 0))],
        out_specs=rows,
        compiler_params=pltpu.CompilerParams(dimension_semantics=("arbitrary",),
                                             vmem_limit_bytes=VMEM_LIMIT),
        name="final_norm",
    )(act_a, w_a, gates, y_b, w_o, x2, norm_w.reshape(1, d))


def _layer(x2, batch, seq, norm_w, w_in, conv_a_w, conv_a_b, ln_a_w, ln_a_b, w_a_out,
           conv_b_w, conv_b_b, dt_bias, a_log, d_skip, ssm_norm_w, w_b_out):
    w_in_t = w_in.T
    h, colv, et, rowT = _rmsnorm_and_scalars(x2, norm_w, w_in_t, OFF_S_DT, dt_bias, a_log)
    u = _proj(h, w_in_t, OFF_A_VAL, CONV_CH, "glu", gate_col0=OFF_A_GATE, tm=2048, tn=512)
    za = _proj(h, w_in_t, OFF_A_Z, CONV_CH, "silu")
    zs = _proj(h, w_in_t, OFF_S_Z, D_INNER, "silu", split_w=GROUP_W)
    xs = _proj(h, w_in_t, OFF_S_X, D_INNER, "id", split_w=GROUP_W)
    assert OFF_S_C == OFF_S_B + N_GROUPS * D_STATE
    bcm = _proj(h, w_in_t, OFF_S_B, 2 * N_GROUPS * D_STATE, "id", split_w=D_STATE)
    gates = _proj(h, w_in_t, OFF_G, 2 * D_MODEL, "sigmoid")

    act_a = _conformer_pre(u, za, conv_a_w, conv_a_b, ln_a_w, ln_a_b, batch, seq)
    y_b = _ssd_branch(xs, bcm, zs, colv, et, rowT, conv_b_w, conv_b_b, d_skip, ssm_norm_w,
                      w_b_out.astype(BF16), gates, 1, seq)
    return act_a, gates, y_b


def kernel(x, norm_w, w_in, conv_a_w, conv_a_b, ln_a_w, ln_a_b, w_a_out, conv_b_w, conv_b_b, dt_bias, a_log,
           d_skip, ssm_norm_w, w_b_out, w_o, final_norm_w):
    batch, seq, d = x.shape
    depth = norm_w.shape[0]
    assert depth == 1, "the residual add is fused with the final norm for a single layer"
    x2 = x.reshape(batch * seq, d)
    act_a, gates, y_b = _layer(x2, batch, seq, norm_w[0], w_in[0], conv_a_w[0], conv_a_b[0], ln_a_w[0],
                               ln_a_b[0], w_a_out[0], conv_b_w[0], conv_b_b[0], dt_bias[0], a_log[0],
                               d_skip[0], ssm_norm_w[0], w_b_out[0])
    out = _final(act_a, w_a_out[0].astype(BF16), gates, y_b, w_o[0].astype(BF16), x2, final_norm_w)
    return out.reshape(batch, seq, d)
```

```python
import functools
import math

import jax
import jax.numpy as jnp
from jax import lax
from jax.experimental import pallas as pl
from jax.experimental.pallas import tpu as pltpu

F32 = jnp.float32
BF16 = jnp.bfloat16

D_MODEL = 2048
CONV_CH = D_MODEL
CONV_WIDTH = 31
D_INNER = 2 * D_MODEL
HEADDIM = 64
N_HEADS = D_INNER // HEADDIM
D_STATE = 128
N_GROUPS = 8
HEADS_PER_GROUP = N_HEADS // N_GROUPS
GROUP_W = HEADS_PER_GROUP * HEADDIM
SSM_CONV = 4
CHUNK = 128
XBC_DIM = D_INNER + 2 * N_GROUPS * D_STATE
NORM_EPS = 1e-6
LN_EPS = 1e-5
LOG2E = math.log2(math.e)

OFF_A_VAL = 0
OFF_A_GATE = OFF_A_VAL + CONV_CH
OFF_A_Z = OFF_A_GATE + CONV_CH
OFF_S_Z = OFF_A_Z + CONV_CH
OFF_S_X = OFF_S_Z + D_INNER
OFF_S_B = OFF_S_X + D_INNER
OFF_S_C = OFF_S_B + N_GROUPS * D_STATE
OFF_S_DT = OFF_S_C + N_GROUPS * D_STATE
OFF_G = OFF_S_DT + N_HEADS

LANES = 128
CONV_HALO = 32
TAIL_ROWS = 8
VMEM_LIMIT = 56 * 1024 * 1024


def _sigmoid(v):
    return 0.5 * jnp.tanh(0.5 * v) + 0.5


def _silu(v):
    t = 0.5 * v
    return t * jnp.tanh(t) + t


def _softplus(v):
    return jnp.maximum(v, 0.0) + jnp.log1p(jnp.exp(-jnp.abs(v)))


def _split2(v):
    v0 = v.astype(BF16)
    return v0, (v - v0.astype(F32)).astype(BF16)


def _split3(v):
    v0 = v.astype(BF16)
    r = v - v0.astype(F32)
    v1 = r.astype(BF16)
    r = r - v1.astype(F32)
    return v0, v1, r.astype(BF16)


def _dot(a, b):
    return jnp.dot(a, b, preferred_element_type=F32)


def _dot_nt(a, b):
    return lax.dot_general(a, b, (((1,), (1,)), ((), ())), preferred_element_type=F32)


def _ssd_scalars(h, w_dt, bias, a_log, col_ref, et_ref, rowT_ref):
    tm = h.shape[0]
    nh = bias.shape[1]
    dt = _softplus(_dot_nt(h, w_dt) + bias)
    da = dt * (-jnp.exp(a_log))
    log2dt = jnp.log(dt) * LOG2E
    row = lax.broadcasted_iota(jnp.int32, (CHUNK, CHUNK), 0)
    col = lax.broadcasted_iota(jnp.int32, (CHUNK, CHUNK), 1)
    tril = (row >= col).astype(BF16)
    for c in range(tm // CHUNK):
        sl = slice(c * CHUNK, (c + 1) * CHUNK)
        d0, d1, d2 = _split3(da[sl])
        cum = _dot(tril, d0) + _dot(tril, d1) + _dot(tril, d2)
        cum2 = cum * LOG2E
        col_ref[sl, :] = cum2
        e_hi, e_lo = _split2(jnp.exp(cum))
        t_hi, t_lo = _split2(jnp.exp(cum[CHUNK - 1:CHUNK, :] - cum) * dt[sl])
        et_ref[sl, :] = jnp.concatenate([e_hi, t_hi, e_lo, t_lo], axis=1)
        both = jnp.concatenate([cum2 - log2dt[sl], cum2], axis=1)
        rowT_ref[:, sl] = both.T[:nh]


def _rmsnorm_kernel(x_ref, w_ref, wdt_ref, bias_ref, alog_ref, o_ref, col_ref, et_ref, rowT_ref):
    x = x_ref[...]
    ms = jnp.mean(x * x, axis=-1, keepdims=True)
    h = (x * lax.rsqrt(ms + NORM_EPS) * w_ref[...]).astype(o_ref.dtype)
    o_ref[...] = h
    _ssd_scalars(h, wdt_ref[...].astype(BF16), bias_ref[...], alog_ref[...], col_ref, et_ref, rowT_ref)


def _rmsnorm_and_scalars(x2, w, wt, col0, dt_bias, a_log, tm=1024):
    m, d = x2.shape
    nh = dt_bias.shape[0]
    assert col0 % nh == 0 and tm % CHUNK == 0

    def vec(n):
        return pl.BlockSpec((1, n), lambda i: (0, 0))

    return pl.pallas_call(
        _rmsnorm_kernel,
        out_shape=(jax.ShapeDtypeStruct((m, d), BF16),
                   jax.ShapeDtypeStruct((m, nh), F32),
                   jax.ShapeDtypeStruct((m, 4 * nh), BF16),
                   jax.ShapeDtypeStruct((nh, m), F32)),
        grid=(m // tm,),
        in_specs=[pl.BlockSpec((tm, d), lambda i: (i, 0)), vec(d),
                  pl.BlockSpec((nh, d), lambda i: (col0 // nh, 0)), vec(nh), vec(nh)],
        out_specs=(pl.BlockSpec((tm, d), lambda i: (i, 0)),
                   pl.BlockSpec((tm, nh), lambda i: (i, 0)),
                   pl.BlockSpec((tm, 4 * nh), lambda i: (i, 0)),
                   pl.BlockSpec((nh, tm), lambda i: (0, i))),
        compiler_params=pltpu.CompilerParams(dimension_semantics=("arbitrary",),
                                             vmem_limit_bytes=VMEM_LIMIT),
        name="rmsnorm_scalars",
    )(x2, w.reshape(1, d), wt, dt_bias.reshape(1, nh), a_log.reshape(1, nh))


def _proj_kernel(*refs, mode, nsplit, shift):
    if mode == "glu":
        h_ref, w_ref, w2_ref, o_ref, w16_ref, w216_ref = refs
    elif shift:
        h_ref, w_ref, w2_ref, o_ref, w16_ref = refs
    else:
        h_ref, w_ref, o_ref, w16_ref = refs

    @pl.when(pl.program_id(1) == 0)
    def _():
        if shift:
            w16_ref[...] = jnp.concatenate([w_ref[shift:, :], w2_ref[...]], axis=0).astype(BF16)
        else:
            w16_ref[...] = w_ref[...].astype(BF16)
        if mode == "glu":
            w216_ref[...] = w2_ref[...].astype(BF16)

    h = h_ref[...]
    acc = _dot_nt(h, w16_ref[...])
    if mode == "glu":
        res = acc * _sigmoid(_dot_nt(h, w216_ref[...]))
    elif mode == "silu":
        res = _silu(acc)
    elif mode == "sigmoid":
        res = _sigmoid(acc)
    else:
        res = acc
    width = res.shape[1] // nsplit
    for s in range(nsplit):
        o_ref[s] = res[:, s * width:(s + 1) * width].astype(o_ref.dtype)


def _proj(h, wt, col0, ncols, mode, *, split_w=None, gate_col0=None, tm=1024, tn=1024):
    m, k = h.shape
    cb0, shift = divmod(col0, tn)
    in_specs = [pl.BlockSpec((tm, k), lambda j, i: (i, 0)),
                pl.BlockSpec((tn, k), lambda j, i: (cb0 + j, 0))]
    args = [h, wt]
    scratch = [pltpu.VMEM((tn, k), BF16)]
    if shift:
        assert mode != "glu" and shift % 8 == 0 and tn % shift == 0
        in_specs.append(pl.BlockSpec((shift, k), lambda j, i: ((cb0 + j + 1) * (tn // shift), 0)))
        args.append(wt)
    if mode == "glu":
        gb0 = gate_col0 // tn
        in_specs.append(pl.BlockSpec((tn, k), lambda j, i: (gb0 + j, 0)))
        args.append(wt)
        scratch.append(pltpu.VMEM((tn, k), BF16))
    if split_w is None:
        nsplit = 1
        out_shape = jax.ShapeDtypeStruct((1, m, ncols), BF16)
        out_spec = pl.BlockSpec((1, tm, tn), lambda j, i: (0, i, j))
    else:
        nsplit = tn // split_w
        out_shape = jax.ShapeDtypeStruct((ncols // split_w, m, split_w), BF16)
        out_spec = pl.BlockSpec((nsplit, tm, split_w), lambda j, i: (j, i, 0))
    out = pl.pallas_call(
        functools.partial(_proj_kernel, mode=mode, nsplit=nsplit, shift=shift),
        out_shape=out_shape,
        grid=(ncols // tn, m // tm),
        in_specs=in_specs,
        out_specs=out_spec,
        scratch_shapes=scratch,
        compiler_params=pltpu.CompilerParams(dimension_semantics=("arbitrary", "arbitrary"),
                                             vmem_limit_bytes=VMEM_LIMIT),
        name="proj_" + mode,
    )(*args)
    return out[0] if split_w is None else out


def _conf_kernel(u_ref, za_ref, cw_ref, lnw_ref, lnb_ref, o_ref, ubuf, cbuf, *, rows, conv_rows, ln_rows):
    nblk = ubuf.shape[0]

    @pl.when(pl.program_id(1) == 0)
    def _():
        ubuf[:, 0:CONV_HALO, :] = jnp.zeros((nblk, CONV_HALO, LANES), F32)

    for j in range(nblk):
        ubuf[j, CONV_HALO:CONV_HALO + rows, :] = u_ref[:, j * LANES:(j + 1) * LANES].astype(F32)

    first = CONV_HALO - (CONV_WIDTH - 1)

    def conv_block(j, carry):
        w = cw_ref[j]
        for r0 in range(0, rows, conv_rows):
            acc = jnp.broadcast_to(w[CONV_WIDTH:CONV_WIDTH + 1, :], (conv_rows, LANES))
            for k in range(CONV_WIDTH):
                acc = acc + w[k:k + 1, :] * ubuf[j, pl.ds(r0 + first + k, conv_rows), :]
            cbuf[j, r0:r0 + conv_rows, :] = acc
        ubuf[j, 0:CONV_HALO, :] = ubuf[j, rows:rows + CONV_HALO, :]
        return carry

    lax.fori_loop(0, nblk, conv_block, 0)

    inv_c = 1.0 / (nblk * LANES)

    def ln_block(i, carry):
        r0 = pl.multiple_of(i * ln_rows, ln_rows)
        rows_i = pl.ds(r0, ln_rows)
        total = cbuf[0, rows_i, :]
        for j in range(1, nblk):
            total = total + cbuf[j, rows_i, :]
        mu = jnp.sum(total, axis=-1, keepdims=True) * inv_c
        sq = jnp.zeros((ln_rows, LANES), F32)
        for j in range(nblk):
            d = cbuf[j, rows_i, :] - mu
            sq = sq + d * d
        rstd = lax.rsqrt(jnp.sum(sq, axis=-1, keepdims=True) * inv_c + LN_EPS)
        for j in range(nblk):
            lanes = slice(j * LANES, (j + 1) * LANES)
            v = (cbuf[j, rows_i, :] - mu) * rstd * lnw_ref[:, lanes] + lnb_ref[:, lanes]
            v = _silu(v) * za_ref[rows_i, lanes].astype(F32)
            o_ref[rows_i, lanes] = v.astype(o_ref.dtype)
        return carry

    lax.fori_loop(0, rows // ln_rows, ln_block, 0, unroll=2)


def _conformer_pre(u, za, conv_w, conv_b, ln_w, ln_b, batch, seq, rows=1024, conv_rows=64, ln_rows=64):
    m, c = u.shape
    nblk = c // LANES
    assert conv_w.shape[0] == CONV_WIDTH and CONV_WIDTH + 1 <= CONV_HALO
    cw = jnp.concatenate([conv_w, conv_b.reshape(1, c)], axis=0)
    cw = jnp.pad(cw, ((0, CONV_HALO - CONV_WIDTH - 1), (0, 0))).reshape(CONV_HALO, nblk, LANES).transpose(1, 0, 2)
    nt = seq // rows
    tile = pl.BlockSpec((rows, c), lambda b, t: (b * nt + t, 0))
    vec = pl.BlockSpec((1, c), lambda b, t: (0, 0))
    return pl.pallas_call(
        functools.partial(_conf_kernel, rows=rows, conv_rows=conv_rows, ln_rows=ln_rows),
        out_shape=jax.ShapeDtypeStruct((m, c), BF16),
        grid=(batch, nt),
        in_specs=[tile, tile,
                  pl.BlockSpec((nblk, CONV_HALO, LANES), lambda b, t: (0, 0, 0)),
                  vec, vec],
        out_specs=tile,
        scratch_shapes=[pltpu.VMEM((nblk, CONV_HALO + rows, LANES), F32),
                        pltpu.VMEM((nblk, rows, LANES), F32)],
        compiler_params=pltpu.CompilerParams(dimension_semantics=("arbitrary", "arbitrary"),
                                             vmem_limit_bytes=VMEM_LIMIT),
        name="conformer_pre",
    )(u, za, cw, ln_w.reshape(1, c), ln_b.reshape(1, c))


SSD_ROWS = 2 * CHUNK


def _short_conv_silu(buf_ref, w_ref, b_ref, g, row0):
    w = w_ref[g]
    bias = b_ref[g]
    cols = []
    for c in range(buf_ref.shape[1]):
        lanes = slice(c * LANES, (c + 1) * LANES)
        acc = jnp.broadcast_to(bias[:, lanes], (CHUNK, LANES))
        for k in range(SSM_CONV):
            start = row0 + TAIL_ROWS - (SSM_CONV - 1) + k
            acc = acc + w[k:k + 1, lanes] * buf_ref[g, c, pl.ds(start, CHUNK), :]
        cols.append(acc)
    return _silu(jnp.concatenate(cols, axis=1))


def _ssd_kernel(x_ref, b_ref, c_ref, zs_ref, col_ref, et_ref, rowT_ref, expand_ref,
                wx_ref, bx_ref, wb_ref, bb_ref, wc_ref, bc_ref, dsk_ref, nw_ref, wout_ref, gate_ref,
                o_ref,
                state_ref, xbuf, bbuf, cbuf, ynew_ref, yprev_ref, *, steps_per_seq):
    L = CHUNK
    hpg = HEADS_PER_GROUP
    t = pl.program_id(0)

    @pl.when(t % steps_per_seq == 0)
    def _():
        state_ref[...] = jnp.zeros(state_ref.shape, F32)
        for buf in (xbuf, bbuf, cbuf):
            buf[:, :, 0:TAIL_ROWS, :] = jnp.zeros(buf.shape[:2] + (TAIL_ROWS, LANES), F32)

    @pl.when(t == 0)
    def _():
        ynew_ref[...] = jnp.zeros(ynew_ref.shape, ynew_ref.dtype)

    yprev_ref[...] = ynew_ref[...]
    slice_w = o_ref.shape[1] // N_GROUPS

    def out_proj_slice(i):
        cols = slice(i * slice_w, (i + 1) * slice_w)
        proj = _dot(yprev_ref[...], wout_ref[:, cols])
        o_ref[:, cols] = (gate_ref[:, cols].astype(F32) * proj).astype(o_ref.dtype)

    for g in range(N_GROUPS):
        for src, buf in ((x_ref, xbuf), (b_ref, bbuf), (c_ref, cbuf)):
            for c in range(buf.shape[1]):
                buf[g, c, TAIL_ROWS:TAIL_ROWS + SSD_ROWS, :] = src[g, :, c * LANES:(c + 1) * LANES].astype(F32)

    row = lax.broadcasted_iota(jnp.int32, (L, L), 0)
    col = lax.broadcasted_iota(jnp.int32, (L, L), 1)
    causal = row >= col
    low_half = col < HEADDIM

    for g in range(N_GROUPS):
        out_proj_slice(g)
        et_all = _dot(et_ref[...], expand_ref[g])
        for ci in range(SSD_ROWS // L):
            row0 = ci * L
            rows = slice(row0, row0 + L)
            colv = col_ref[rows, :]
            xs = _short_conv_silu(xbuf, wx_ref, bx_ref, g, row0)
            bm = _short_conv_silu(bbuf, wb_ref, bb_ref, g, row0)
            cm = _short_conv_silu(cbuf, wc_ref, bc_ref, g, row0)
            bt = bm.T.astype(BF16)
            cm16 = cm.astype(BF16)
            cb16 = _dot(cm16, bt).astype(BF16)
            st = state_ref[g]
            y_state = _dot(cm16, st.astype(BF16))
            ecol = et_all[rows, :GROUP_W]
            tcol = et_all[rows, GROUP_W:]
            ys = []
            for q in range(hpg // 2):
                xp = xs[:, q * LANES:(q + 1) * LANES]
                ms = []
                for e in range(2):
                    h = g * hpg + 2 * q + e
                    seg = jnp.broadcast_to(colv[:, h:h + 1], (L, L)) - rowT_ref[h:h + 1, rows]
                    decay = jnp.exp2(jnp.where(causal, seg, -jnp.inf))
                    ms.append(cb16 * decay.astype(BF16))
                lhs = jnp.concatenate(ms, axis=1)
                rhs = jnp.concatenate([jnp.where(low_half, xp, 0.0), jnp.where(low_half, 0.0, xp)],
                                      axis=0).astype(BF16)
                ys.append(_dot(lhs, rhs))
            y = jnp.concatenate(ys, axis=1) + y_state * ecol
            state_ref[g] = st * ecol[L - 1:L, :] + _dot(bt, (xs * tcol).astype(BF16))
            y = y + dsk_ref[g] * xs
            y = y * zs_ref[g, rows, :].astype(F32)
            y = y * lax.rsqrt(jnp.mean(y * y, axis=-1, keepdims=True) + NORM_EPS)
            ynew_ref[rows, g * GROUP_W:(g + 1) * GROUP_W] = (y * nw_ref[g]).astype(ynew_ref.dtype)

    for buf in (xbuf, bbuf, cbuf):
        buf[:, :, 0:TAIL_ROWS, :] = buf[:, :, SSD_ROWS:SSD_ROWS + TAIL_ROWS, :]


def _head_expand_matrix():
    src = jnp.arange(4 * N_HEADS)
    head, is_t = src % N_HEADS, (src // N_HEADS) % 2
    dst = jnp.arange(2 * GROUP_W)
    dst_t, dst_head_in_group = dst // GROUP_W, (dst % GROUP_W) // HEADDIM
    g = jnp.arange(N_GROUPS)[:, None, None]
    hit = (head[None, :, None] == g * HEADS_PER_GROUP + dst_head_in_group[None, None, :]) \
        & (is_t[None, :, None] == dst_t[None, None, :])
    return hit.astype(BF16)


def _ssd_branch(xs, bcm, zs, colv, et, rowT, conv_w, conv_b, d_skip, norm_w, w_out, gates, gate_col, seq):
    g, m, gw = xs.shape
    n = bcm.shape[2]
    nh = colv.shape[1]
    d_in, d_out = w_out.shape
    nblocks = m // SSD_ROWS
    last = nblocks - 1

    def cur(t):
        return jnp.minimum(t, last)

    def prev(t):
        return jnp.maximum(t - 1, 0)

    def grp(width):
        return pl.BlockSpec((g, SSD_ROWS, width), lambda t: (0, cur(t), 0))

    def const(shape):
        return pl.BlockSpec(shape, lambda t: (0,) * len(shape), pipeline_mode=pl.Buffered(1))

    def group_major(v, width):
        return v.reshape(v.shape[0], g, width).transpose(1, 0, 2)

    wx = group_major(conv_w[:, :D_INNER], gw)
    wb = group_major(conv_w[:, D_INNER:D_INNER + g * n], n)
    wc = group_major(conv_w[:, D_INNER + g * n:], n)
    cbias = conv_b.reshape(1, -1)
    bx = group_major(cbias[:, :D_INNER], gw)
    bb = group_major(cbias[:, D_INNER:D_INNER + g * n], n)
    bc = group_major(cbias[:, D_INNER + g * n:], n)
    dsk = jnp.repeat(d_skip, HEADDIM).reshape(g, 1, gw)
    nw = norm_w.reshape(g, 1, gw)
    return pl.pallas_call(
        functools.partial(_ssd_kernel, steps_per_seq=seq // SSD_ROWS),
        out_shape=jax.ShapeDtypeStruct((m, d_out), BF16),
        grid=(nblocks + 1,),
        in_specs=[grp(gw),
                  pl.BlockSpec((g, SSD_ROWS, n), lambda t: (0, cur(t), 0)),
                  pl.BlockSpec((g, SSD_ROWS, n), lambda t: (1, cur(t), 0)),
                  grp(gw),
                  pl.BlockSpec((SSD_ROWS, nh), lambda t: (cur(t), 0)),
                  pl.BlockSpec((SSD_ROWS, 4 * nh), lambda t: (cur(t), 0)),
                  pl.BlockSpec((nh, SSD_ROWS), lambda t: (0, cur(t))),
                  const((g, 4 * nh, 2 * gw)),
                  const((g, SSM_CONV, gw)), const((g, 1, gw)),
                  const((g, SSM_CONV, n)), const((g, 1, n)),
                  const((g, SSM_CONV, n)), const((g, 1, n)),
                  const((g, 1, gw)), const((g, 1, gw)),
                  const((d_in, d_out)),
                  pl.BlockSpec((SSD_ROWS, d_out), lambda t: (prev(t), gate_col))],
        out_specs=pl.BlockSpec((SSD_ROWS, d_out), lambda t: (prev(t), 0)),
        scratch_shapes=[pltpu.VMEM((g, n, gw), F32),
                        pltpu.VMEM((g, gw // LANES, TAIL_ROWS + SSD_ROWS, LANES), F32),
                        pltpu.VMEM((g, n // LANES, TAIL_ROWS + SSD_ROWS, LANES), F32),
                        pltpu.VMEM((g, n // LANES, TAIL_ROWS + SSD_ROWS, LANES), F32),
                        pltpu.VMEM((SSD_ROWS, d_in), BF16),
                        pltpu.VMEM((SSD_ROWS, d_in), BF16)],
        compiler_params=pltpu.CompilerParams(dimension_semantics=("arbitrary",),
                                             vmem_limit_bytes=VMEM_LIMIT),
        name="ssd_branch",
    )(xs, bcm, bcm, zs, colv, et, rowT, _head_expand_matrix(), wx, bx, wb, bb, wc, bc, dsk, nw, w_out,
      gates)


def _final_kernel(a_ref, wa_ref, ga_ref, yb_ref, wo_ref, x_ref, nw_ref, o_ref):
    merged = ga_ref[...].astype(F32) * _dot(a_ref[...], wa_ref[...]) + yb_ref[...].astype(F32)
    x = x_ref[...] + _dot(merged.astype(BF16), wo_ref[...])
    ms = jnp.mean(x * x, axis=-1, keepdims=True)
    o_ref[...] = x * lax.rsqrt(ms + NORM_EPS) * nw_ref[...]


def _final(act_a, w_a, gates, y_b, w_o, x2, norm_w, tm=256):
    m, d = x2.shape
    rows = pl.BlockSpec((tm, d), lambda i: (i, 0))
    weight = pl.BlockSpec((d, d), lambda i: (0, 0), pipeline_mode=pl.Buffered(1))
    return pl.pallas_call(
        _final_kernel,
        out_shape=jax.ShapeDtypeStruct((m, d), F32),
        grid=(m // tm,),
        in_specs=[rows, weight, rows, rows, weight, rows,
                  pl.BlockSpec((1, d), lambda i: (0, 0))],
        out_specs=rows,
        compiler_params=pltpu.CompilerParams(dimension_semantics=("arbitrary",),
                                             vmem_limit_bytes=VMEM_LIMIT),
        name="final_norm",
    )(act_a, w_a, gates, y_b, w_o, x2, norm_w.reshape(1, d))


def _layer(x2, batch, seq, norm_w, w_in, conv_a_w, conv_a_b, ln_a_w, ln_a_b, w_a_out,
           conv_b_w, conv_b_b, dt_bias, a_log, d_skip, ssm_norm_w, w_b_out):
    w_in_t = w_in.T
    h, colv, et, rowT = _rmsnorm_and_scalars(x2, norm_w, w_in_t, OFF_S_DT, dt_bias, a_log)
    u = _proj(h, w_in_t, OFF_A_VAL, CONV_CH, "glu", gate_col0=OFF_A_GATE, tm=2048, tn=512)
    za = _proj(h, w_in_t, OFF_A_Z, CONV_CH, "silu", tm=2048)
    zs = _proj(h, w_in_t, OFF_S_Z, D_INNER, "silu", split_w=GROUP_W, tm=2048)
    xs = _proj(h, w_in_t, OFF_S_X, D_INNER, "id", split_w=GROUP_W, tm=2048)
    assert OFF_S_C == OFF_S_B + N_GROUPS * D_STATE
    bcm = _proj(h, w_in_t, OFF_S_B, 2 * N_GROUPS * D_STATE, "id", split_w=D_STATE, tm=2048)
    gates = _proj(h, w_in_t, OFF_G, 2 * D_MODEL, "sigmoid")

    act_a = _conformer_pre(u, za, conv_a_w, conv_a_b, ln_a_w, ln_a_b, batch, seq)
    y_b = _ssd_branch(xs, bcm, zs, colv, et, rowT, conv_b_w, conv_b_b, d_skip, ssm_norm_w,
                      w_b_out.astype(BF16), gates, 1, seq)
    return act_a, gates, y_b


def kernel(x, norm_w, w_in, conv_a_w, conv_a_b, ln_a_w, ln_a_b, w_a_out, conv_b_w, conv_b_b, dt_bias, a_log,
           d_skip, ssm_norm_w, w_b_out, w_o, final_norm_w):
    batch, seq, d = x.shape
    depth = norm_w.shape[0]
    assert depth == 1, "the residual add is fused with the final norm for a single layer"
    x2 = x.reshape(batch * seq, d)
    act_a, gates, y_b = _layer(x2, batch, seq, norm_w[0], w_in[0], conv_a_w[0], conv_a_b[0], ln_a_w[0],
                               ln_a_b[0], w_a_out[0], conv_b_w[0], conv_b_b[0], dt_bias[0], a_log[0],
                               d_skip[0], ssm_norm_w[0], w_b_out[0])
    out = _final(act_a, w_a_out[0].astype(BF16), gates, y_b, w_o[0].astype(BF16), x2, final_norm_w)
    return out.reshape(batch, seq, d)
```
